```python
import math, functools
import jax, jax.numpy as jnp
from jax import lax
import numpy as np

D_MODEL = 1024
BATCH = 4
SEQ = 4096
DEPTH = 1
DEC_BATCH = 128
DEC_SEQ = 4
PAST_LEN = 8192
PAGE_SIZE = 128

M_HEADS = 4
M_DIM = 256
M_WIDTH = M_HEADS * M_DIM
M_CHUNK = 64
F_BIAS_LO = 3.0
F_BIAS_HI = 6.0
A_HEADS = 8
A_KV_HEADS = 4
A_DIM = 128
A_WIDTH = A_HEADS * A_DIM
A_KV_WIDTH = A_KV_HEADS * A_DIM
IDX_HEADS = 4
IDX_DIM = 64
TOPK_MAX = 256
Q_BLOCK = 128
REL_BUCKETS = 32
REL_MAX_DIST = 128
EPS = 1e-6

SPLITS = (
    ('m_q', M_WIDTH), ('m_k', M_WIDTH), ('m_v', M_WIDTH), ('m_o', M_WIDTH), ('m_z', M_WIDTH),
    ('m_i', M_HEADS), ('m_f', M_HEADS),
    ('a_q', A_WIDTH), ('a_k', A_KV_WIDTH), ('a_v', A_KV_WIDTH), ('a_z', A_WIDTH),
    ('ix_q', IDX_HEADS * IDX_DIM), ('ix_k', IDX_DIM), ('ix_w', IDX_HEADS),
    ('g_m', D_MODEL), ('g_a', D_MODEL),
)
IN_WIDTH = sum(w for _, w in SPLITS)

kernel_name = 'hybrid_mlstm_dsa_decode_step'

f32 = jnp.float32


def _rms(x, g):
    x = x.astype(f32)
    return x * lax.rsqrt(jnp.mean(x * x, axis=-1, keepdims=True) + EPS) * g.astype(f32)


def _split(u):
    out, off = {}, 0
    for name, w in SPLITS:
        out[name] = u[..., off:off + w]
        off += w
    return out


def _take_rows(a, idx):
    return jax.vmap(lambda a_, i_: a_[i_])(a, idx)


def _rel_bucket(dist):
    max_exact = REL_BUCKETS // 2
    d = jnp.maximum(dist.astype(f32), 1.0)
    large = max_exact + (jnp.log(d / max_exact) / math.log(REL_MAX_DIST / max_exact)
                         * (REL_BUCKETS - max_exact)).astype(jnp.int32)
    large = jnp.minimum(large, REL_BUCKETS - 1)
    return jnp.where(dist < max_exact, dist, large)


def _mlstm_chunk(carry, inp):
    C, n, m = carry
    q, k, v, ig, lf = inp
    L = q.shape[2]
    b = jnp.cumsum(lf, axis=-1)
    causal = jnp.tril(jnp.ones((L, L), bool))
    dmat = jnp.where(causal, b[..., :, None] - b[..., None, :] + ig[..., None, :], -jnp.inf)
    inter = b + m[..., None]
    m_t = jnp.maximum(inter, jnp.max(dmat, axis=-1))
    w_intra = jnp.exp(dmat - m_t[..., None])
    w_inter = jnp.exp(inter - m_t)
    s = jnp.einsum('bhtd,bhsd->bhts', q, k) * w_intra
    num = w_inter[..., None] * jnp.einsum('bhtd,bhde->bhte', q, C) + jnp.einsum('bhts,bhse->bhte', s, v)
    den = w_inter * jnp.einsum('bhtd,bhd->bht', q, n) + jnp.sum(s, axis=-1)
    h = num / jnp.maximum(jnp.abs(den), jnp.exp(-m_t))[..., None]
    bL = b[..., -1]
    dec = bL[..., None] - b + ig
    m_new = jnp.maximum(bL + m, jnp.max(dec, axis=-1))
    wk = jnp.exp(dec - m_new[..., None])
    sc = jnp.exp(bL + m - m_new)
    C_new = sc[..., None, None] * C + jnp.einsum('bhs,bhsd,bhse->bhde', wk, k, v)
    n_new = sc[..., None] * n + jnp.einsum('bhs,bhsd->bhd', wk, k)
    return (C_new, n_new, m_new), h


def _mlstm_branch(u, b_if, g_mlstm, C0, n0, m0):
    Bz, T, _ = u['m_q'].shape

    def heads(a):
        return a.reshape(Bz, T, M_HEADS, M_DIM).transpose(0, 2, 1, 3).astype(f32)

    q = heads(u['m_q'])
    k = heads(u['m_k']) * (M_DIM ** -0.5)
    v = heads(u['m_v'])
    b_if = b_if.astype(f32)
    ig = (u['m_i'].astype(f32) + b_if[:M_HEADS]).transpose(0, 2, 1)
    lf = jax.nn.log_sigmoid(u['m_f'].astype(f32) + b_if[M_HEADS:]).transpose(0, 2, 1)
    L = M_CHUNK if T % M_CHUNK == 0 else T
    nc = T // L

    def chunks(a):
        return jnp.moveaxis(a.reshape((Bz, M_HEADS, nc, L) + a.shape[3:]), 2, 0)

    (C, n, m), h = lax.scan(_mlstm_chunk, (C0, n0, m0),
                            (chunks(q), chunks(k), chunks(v), chunks(ig), chunks(lf)))
    h = jnp.moveaxis(h, 0, 2).reshape(Bz, M_HEADS, T, M_DIM).transpose(0, 2, 1, 3)
    h = _rms(h, g_mlstm.reshape(M_HEADS, M_DIM)).reshape(Bz, T, M_WIDTH)
    h = h * jax.nn.sigmoid(u['m_o'].astype(f32)) * jax.nn.silu(u['m_z'].astype(f32))
    return h, (C, n, m)


def _dsa_project(u, g_q, g_k):
    Bz, T, _ = u['a_q'].shape
    q = _rms(u['a_q'].reshape(Bz, T, A_HEADS, A_DIM), g_q)
    k = _rms(u['a_k'].reshape(Bz, T, A_KV_HEADS, A_DIM), g_k)
    v = u['a_v'].reshape(Bz, T, A_KV_HEADS, A_DIM).astype(f32)
    iq = u['ix_q'].reshape(Bz, T, IDX_HEADS, IDX_DIM).astype(f32)
    ik = u['ix_k'].astype(f32)
    iw = u['ix_w'].astype(f32)
    return q, k, v, iq, ik, iw


def _indexer_scores(iq, iw, ik):
    s = jnp.einsum('bthd,bsd->bths', iq, ik)
    return jnp.einsum('bths,bth->bts', jax.nn.relu(s), iw)


def _sparse_attend(q, k_sel, v_sel, dist, rel_bias):
    Bz, T = q.shape[:2]
    G = A_HEADS // A_KV_HEADS
    qg = q.reshape(Bz, T, A_KV_HEADS, G, A_DIM)
    logits = jnp.einsum('btkgd,btskd->btkgs', qg, k_sel) * (A_DIM ** -0.5)
    bias = rel_bias.astype(f32)[_rel_bucket(jnp.maximum(dist, 0))]
    bias = bias.reshape(Bz, T, -1, A_KV_HEADS, G).transpose(0, 1, 3, 4, 2)
    logits = jnp.where((dist >= 0)[:, :, None, None, :], logits + bias, -jnp.inf)
    p = jax.nn.softmax(logits, axis=-1)
    o = jnp.einsum('btkgs,btskd->btkgd', p, v_sel)
    return o.reshape(Bz, T, A_WIDTH)


def _dsa_prompt(u, g_q, g_k, rel_bias):
    q, k, v, iq, ik, iw = _dsa_project(u, g_q, g_k)
    Bz, S = q.shape[:2]
    topk = min(TOPK_MAX, S // 4)
    key_pos = jnp.arange(S)

    def block(i):
        t0 = i * Q_BLOCK
        qb = lax.dynamic_slice_in_dim(q, t0, Q_BLOCK, axis=1)
        iqb = lax.dynamic_slice_in_dim(iq, t0, Q_BLOCK, axis=1)
        iwb = lax.dynamic_slice_in_dim(iw, t0, Q_BLOCK, axis=1)
        pos_q = t0 + jnp.arange(Q_BLOCK)
        sc = _indexer_scores(iqb, iwb, ik)
        sc = jnp.where((key_pos[None, :] <= pos_q[:, None])[None], sc, -jnp.inf)
        _, idx = lax.top_k(sc, topk)
        k_sel = _take_rows(k, idx)
        v_sel = _take_rows(v, idx)
        dist = pos_q[None, :, None] - idx
        return _sparse_attend(qb, k_sel, v_sel, dist, rel_bias)

    o = lax.map(block, jnp.arange(S // Q_BLOCK))
    o = jnp.moveaxis(o, 0, 1).reshape(Bz, S, A_WIDTH)
    return o, k, v, ik


def _dsa_sample(u, g_q, g_k, rel_bias, *, layer, cache_k, cache_v, cache_ik, page_table):
    q, k, v, iq, ik, iw = _dsa_project(u, g_q, g_k)
    DB, T = q.shape[:2]
    past = page_table.shape[1] * PAGE_SIZE
    Ltot = past + T
    topk = min(TOPK_MAX, Ltot // 4)
    ik_past = cache_ik[layer, page_table].reshape(DB, past, IDX_DIM).astype(f32)
    ik_all = jnp.concatenate([ik_past, ik], axis=1)
    pos_q = past + jnp.arange(T)
    sc = _indexer_scores(iq, iw, ik_all)
    sc = jnp.where((jnp.arange(Ltot)[None, :] <= pos_q[:, None])[None], sc, -jnp.inf)
    _, idx = lax.top_k(sc, topk)
    from_past = idx < past
    pidx = jnp.minimum(idx, past - 1)
    page_sel = jnp.take_along_axis(page_table, (pidx // PAGE_SIZE).reshape(DB, -1), axis=1).reshape(idx.shape)
    slot = pidx % PAGE_SIZE
    k_past = cache_k[layer, page_sel, slot].astype(f32)
    v_past = cache_v[layer, page_sel, slot].astype(f32)
    nidx = jnp.clip(idx - past, 0, T - 1)
    sel = from_past[..., None, None]
    k_sel = jnp.where(sel, k_past, _take_rows(k, nidx))
    v_sel = jnp.where(sel, v_past, _take_rows(v, nidx))
    dist = pos_q[None, :, None] - idx
    o = _sparse_attend(q, k_sel, v_sel, dist, rel_bias)
    return o, k, v, ik


def _layer(x, dsa_fn, C0, n0, m0, rel_bias, w_norm, w_in, b_if, g_mlstm, g_q, g_k,
           w_branch_m, w_branch_a, w_out):
    xn = _rms(x, w_norm).astype(x.dtype)
    u = _split(xn @ w_in)
    h_m, (C, n, m) = _mlstm_branch(u, b_if, g_mlstm, C0, n0, m0)
    o_a, k_rows, v_rows, ik_rows = dsa_fn(u, g_q, g_k, rel_bias)
    h_a = o_a * jax.nn.silu(u['a_z'].astype(f32))
    y_m = (h_m.astype(x.dtype) @ w_branch_m).astype(f32)
    y_a = (h_a.astype(x.dtype) @ w_branch_a).astype(f32)
    merged = jax.nn.sigmoid(u['g_m'].astype(f32)) * y_m + jax.nn.sigmoid(u['g_a'].astype(f32)) * y_a
    y = x + (merged.astype(x.dtype) @ w_out).astype(x.dtype)
    return y, (k_rows, v_rows, ik_rows, C, n, m)


def setup_inputs(seed: int = 0) -> dict:
    key = jax.random.key(seed)
    ks = jax.random.split(key, 20)
    n_pages = PAST_LEN // PAGE_SIZE
    n_used = DEC_BATCH * n_pages
    n_pool = n_used + n_used // 4

    def nrm(k, shape, s):
        return s * jax.random.normal(k, shape, f32)

    x_prompt = nrm(ks[0], (BATCH, SEQ, D_MODEL), 1.0)
    x_sample = nrm(ks[1], (DEC_BATCH, DEC_SEQ, D_MODEL), 1.0)
    cache_k = nrm(ks[2], (DEPTH, n_pool, PAGE_SIZE, A_KV_HEADS, A_DIM), 1.0)
    cache_v = nrm(ks[3], (DEPTH, n_pool, PAGE_SIZE, A_KV_HEADS, A_DIM), 1.0)
    cache_idx_k = nrm(ks[4], (DEPTH, n_pool, PAGE_SIZE, IDX_DIM), 1.0)
    state_C = nrm(ks[5], (DEPTH, DEC_BATCH, M_HEADS, M_DIM, M_DIM), 0.05)
    state_n = nrm(ks[6], (DEPTH, DEC_BATCH, M_HEADS, M_DIM), 0.5)
    state_m = nrm(ks[7], (DEPTH, DEC_BATCH, M_HEADS), 0.5)
    page_table = jax.random.permutation(ks[8], n_pool)[:n_used].reshape(DEC_BATCH, n_pages).astype(jnp.int32)
    rel_bias = nrm(ks[9], (REL_BUCKETS, A_HEADS), 0.5)
    w_norm = 1.0 + nrm(ks[10], (DEPTH, D_MODEL), 0.02)
    w_in = nrm(ks[11], (DEPTH, D_MODEL, IN_WIDTH), D_MODEL ** -0.5)
    b_i = nrm(ks[12], (DEPTH, M_HEADS), 0.1)
    b_f = jnp.linspace(F_BIAS_LO, F_BIAS_HI, M_HEADS, dtype=f32)[None, :] + nrm(ks[13], (DEPTH, M_HEADS), 0.1)
    b_if = jnp.concatenate([b_i, b_f], axis=-1)
    g_mlstm = 1.0 + nrm(ks[14], (DEPTH, M_WIDTH), 0.02)
    g_q = 1.0 + nrm(ks[15], (DEPTH, A_DIM), 0.02)
    g_k = 1.0 + nrm(ks[16], (DEPTH, A_DIM), 0.02)
    w_branch_m = nrm(ks[17], (DEPTH, M_WIDTH, D_MODEL), M_WIDTH ** -0.5)
    w_branch_a = nrm(ks[18], (DEPTH, A_WIDTH, D_MODEL), A_WIDTH ** -0.5)
    w_out = nrm(ks[19], (DEPTH, D_MODEL, D_MODEL), D_MODEL ** -0.5)
    return {'x_prompt': x_prompt, 'x_sample': x_sample, 'cache_k': cache_k, 'cache_v': cache_v,
            'cache_idx_k': cache_idx_k, 'state_C': state_C, 'state_n': state_n, 'state_m': state_m,
            'page_table': page_table, 'rel_bias': rel_bias, 'w_norm': w_norm, 'w_in': w_in, 'b_if': b_if,
            'g_mlstm': g_mlstm, 'g_q': g_q, 'g_k': g_k, 'w_branch_m': w_branch_m,
            'w_branch_a': w_branch_a, 'w_out': w_out}


def reference(x_prompt, x_sample, cache_k, cache_v, cache_idx_k, state_C, state_n, state_m, page_table,
              rel_bias, w_norm, w_in, b_if, g_mlstm, g_q, g_k, w_branch_m, w_branch_a, w_out):
    Bp = x_prompt.shape[0]
    yp, ys = x_prompt, x_sample
    new_p = [[] for _ in range(6)]
    new_s = [[] for _ in range(6)]
    for l in range(DEPTH):
        wl = (rel_bias, w_norm[l], w_in[l], b_if[l], g_mlstm[l], g_q[l], g_k[l],
              w_branch_m[l], w_branch_a[l], w_out[l])
        C0 = jnp.zeros((Bp, M_HEADS, M_DIM, M_DIM), f32)
        n0 = jnp.zeros((Bp, M_HEADS, M_DIM), f32)
        m0 = jnp.zeros((Bp, M_HEADS), f32)
        yp, st_p = _layer(yp, _dsa_prompt, C0, n0, m0, *wl)
        dsa_s = functools.partial(_dsa_sample, layer=l, cache_k=cache_k, cache_v=cache_v,
                                  cache_ik=cache_idx_k, page_table=page_table)
        ys, st_s = _layer(ys, dsa_s, state_C[l].astype(f32), state_n[l].astype(f32),
                          state_m[l].astype(f32), *wl)
        for lst, a in zip(new_p, st_p):
            lst.append(a)
        for lst, a in zip(new_s, st_s):
            lst.append(a)
    k_p = jnp.stack(new_p[0]).astype(cache_k.dtype)
    v_p = jnp.stack(new_p[1]).astype(cache_v.dtype)
    ik_p = jnp.stack(new_p[2]).astype(cache_idx_k.dtype)
    C_p = jnp.stack(new_p[3]).astype(state_C.dtype)
    n_p = jnp.stack(new_p[4]).astype(state_n.dtype)
    m_p = jnp.stack(new_p[5]).astype(state_m.dtype)
    k_s = jnp.stack(new_s[0]).astype(cache_k.dtype)
    v_s = jnp.stack(new_s[1]).astype(cache_v.dtype)
    ik_s = jnp.stack(new_s[2]).astype(cache_idx_k.dtype)
    C_s = jnp.stack(new_s[3]).astype(state_C.dtype)
    n_s = jnp.stack(new_s[4]).astype(state_n.dtype)
    m_s = jnp.stack(new_s[5]).astype(state_m.dtype)
    return (yp, ys, k_p, v_p, ik_p, C_p, n_p, m_p, k_s, v_s, ik_s, C_s, n_s, m_s)
```

```python
import functools
import math

import jax
import jax.numpy as jnp
from jax import lax
from jax.experimental import pallas as pl
from jax.experimental.pallas import tpu as pltpu

M_HEADS = 4
M_DIM = 256
M_WIDTH = M_HEADS * M_DIM
A_HEADS = 8
A_KV_HEADS = 4
A_GROUP = A_HEADS // A_KV_HEADS
A_DIM = 128
A_WIDTH = A_HEADS * A_DIM
A_KV_WIDTH = A_KV_HEADS * A_DIM
IDX_HEADS = 4
IDX_DIM = 64
TOPK_MAX = 256
Q_BLOCK = 128
PAGE_SIZE = 128
REL_BUCKETS = 32
REL_MAX_DIST = 128
EPS = 1e-6

LANE = 128
SUBLANE = 8
VMEM_LIMIT = 56 * 1024 * 1024

SM_IK = 0
SM_I = IDX_DIM
SM_F = SM_I + M_HEADS
SM_W = SM_F + M_HEADS

f32 = jnp.float32
bf16 = jnp.bfloat16
i32 = jnp.int32

NEG_INF = float("-inf")
KEY_NEG_INF = -2139095041
KEY_UNSELECTED = KEY_NEG_INF
INT_MIN = -2147483648
INT_MAX = 2147483647


def _cparams(sem):
    return pltpu.CompilerParams(dimension_semantics=sem, vmem_limit_bytes=VMEM_LIMIT)


def _sigmoid(x):
    return 1.0 / (1.0 + jnp.exp(-x))


def _log_sigmoid(x):
    return jnp.minimum(x, 0.0) - jnp.log(1.0 + jnp.exp(-jnp.abs(x)))


def _float_key(x):
    b = lax.bitcast_convert_type(x + 0.0, i32)
    return jnp.where(b < 0, b ^ INT_MAX, b)


def _floor_avg(lo, hi):
    return (lo >> 1) + (hi >> 1) + (lo & hi & 1)


def _rmsnorm_kernel(x_ref, g_ref, o_ref):
    x = x_ref[...]
    ms = jnp.mean(x * x, axis=-1, keepdims=True)
    o_ref[...] = (x * lax.rsqrt(ms + EPS) * g_ref[...]).astype(bf16)


def _rmsnorm(x, g, tm):
    m, d = x.shape
    return pl.pallas_call(
        _rmsnorm_kernel,
        out_shape=jax.ShapeDtypeStruct((m, d), bf16),
        grid=(m // tm,),
        in_specs=[pl.BlockSpec((tm, d), lambda i: (i, 0)),
                  pl.BlockSpec((1, d), lambda i: (0, 0))],
        out_specs=pl.BlockSpec((tm, d), lambda i: (i, 0)),
        compiler_params=_cparams(("parallel",)),
        name="rmsnorm",
    )(x, g.reshape(1, d))


def _proj_cast_kernel(x_ref, w_ref, o_ref):
    o_ref[...] = jnp.dot(x_ref[...], w_ref[...], preferred_element_type=f32).astype(o_ref.dtype)


def _proj_act_kernel(x_ref, w_ref, o_ref, *, act):
    a = jnp.dot(x_ref[...], w_ref[...], preferred_element_type=f32)
    if act == "silu":
        a = a * _sigmoid(a)
    elif act == "sigmoid":
        a = _sigmoid(a)
    o_ref[...] = a


def _proj_oz_kernel(x_ref, wo_ref, wz_ref, o_ref):
    x = x_ref[...]
    o = jnp.dot(x, wo_ref[...], preferred_element_type=f32)
    z = jnp.dot(x, wz_ref[...], preferred_element_type=f32)
    o_ref[...] = _sigmoid(o) * (z * _sigmoid(z))


def _proj_knorm_kernel(x_ref, w_ref, g_ref, o_ref, ob_ref):
    a = jnp.dot(x_ref[...], w_ref[...], preferred_element_type=f32)
    g = g_ref[...]
    for c in range(a.shape[1] // A_DIM):
        blk = a[:, c * A_DIM:(c + 1) * A_DIM]
        ms = jnp.mean(blk * blk, axis=-1, keepdims=True)
        kn = blk * lax.rsqrt(ms + EPS) * g
        o_ref[:, c * A_DIM:(c + 1) * A_DIM] = kn
        ob_ref[:, c * A_DIM:(c + 1) * A_DIM] = kn.astype(bf16)


def _proj_qnorm_kernel(x_ref, w_ref, g_ref, o_ref):
    a = jnp.dot(x_ref[...], w_ref[...], preferred_element_type=f32)
    g = g_ref[...]
    for c in range(a.shape[1] // A_DIM):
        blk = a[:, c * A_DIM:(c + 1) * A_DIM]
        ms = jnp.mean(blk * blk, axis=-1, keepdims=True)
        o_ref[:, c * A_DIM:(c + 1) * A_DIM] = (blk * lax.rsqrt(ms + EPS) * g).astype(o_ref.dtype)


def _proj_nat(body, xn, weights, extras, outs, tm, tn, name):
    m, d = xn.shape
    n = weights[0].shape[1]
    in_specs = [pl.BlockSpec((tm, d), lambda j, i: (i, 0))]
    in_specs += [pl.BlockSpec((d, tn), lambda j, i: (0, j)) for _ in weights]
    in_specs += [pl.BlockSpec((1, e.shape[1]), lambda j, i: (0, 0)) for e in extras]
    out_shape = [jax.ShapeDtypeStruct((m, n), dt) for dt in outs]
    out_specs = [pl.BlockSpec((tm, tn), lambda j, i: (i, j)) for _ in outs]
    res = pl.pallas_call(
        body,
        out_shape=out_shape,
        grid=(n // tn, m // tm),
        in_specs=in_specs,
        out_specs=out_specs,
        compiler_params=_cparams(("parallel", "parallel")),
        name=name,
    )(xn, *weights, *extras)
    return res


def _projT_kernel(x_ref, wt_ref, *rest, kind):
    if kind == "qnorm":
        g_ref, o_ref = rest
    else:
        (o_ref,) = rest
    a = lax.dot_general(wt_ref[...], x_ref[...], (((1,), (1,)), ((), ())),
                        preferred_element_type=f32)
    tn, tm = a.shape
    if kind == "qnorm":
        parts = []
        for c in range(tn // A_DIM):
            blk = a[c * A_DIM:(c + 1) * A_DIM, :]
            ms = jnp.mean(blk * blk, axis=0, keepdims=True)
            parts.append(blk * lax.rsqrt(ms + EPS))
        a = jnp.concatenate(parts, axis=0) if len(parts) > 1 else parts[0]
        g = g_ref[...]
    for c in range(tm // LANE):
        blk = a[:, c * LANE:(c + 1) * LANE]
        if kind == "qnorm":
            blk = blk * g
        o_ref[c] = blk.astype(o_ref.dtype)


def _proj_T(xn, wt, g, kind, out_dtype, tm, tn, name):
    m, d = xn.shape
    n = wt.shape[0]
    in_specs = [pl.BlockSpec((tm, d), lambda j, i: (i, 0)),
                pl.BlockSpec((tn, d), lambda j, i: (j, 0))]
    args = [xn, wt]
    if kind == "qnorm":
        in_specs.append(pl.BlockSpec((tn, LANE), lambda j, i: (j, 0)))
        args.append(g)
    return pl.pallas_call(
        functools.partial(_projT_kernel, kind=kind),
        out_shape=jax.ShapeDtypeStruct((m // LANE, n, LANE), out_dtype),
        grid=(n // tn, m // tm),
        in_specs=in_specs,
        out_specs=pl.BlockSpec((tm // LANE, tn, LANE), lambda j, i: (i, j, 0)),
        compiler_params=_cparams(("parallel", "parallel")),
        name=name,
    )(*args)


def _col_to_row(col, eye):
    return jnp.sum(jnp.where(eye, col, 0.0), axis=0, keepdims=True)


def _mlstm_head(q, k, v, ig_col, lf_col, c_prev, n_prev, m_prev, tri, eye):
    L = q.shape[0]
    ig_row = _col_to_row(ig_col, eye)
    lf_row = _col_to_row(lf_col, eye)
    b_col = jnp.sum(jnp.where(tri, lf_row, 0.0), axis=1, keepdims=True)
    b_row = _col_to_row(b_col, eye)
    dmat = jnp.where(tri, b_col - b_row + ig_row, NEG_INF)
    inter = b_col + m_prev
    m_t = jnp.maximum(inter, jnp.max(dmat, axis=1, keepdims=True))
    w_intra = jnp.exp(dmat - m_t)
    w_inter = jnp.exp(inter - m_t)
    s = lax.dot_general(q, k, (((1,), (1,)), ((), ())), preferred_element_type=f32) * w_intra
    num = w_inter * jnp.dot(q, c_prev.astype(bf16), preferred_element_type=f32)
    num = num + jnp.dot(s.astype(bf16), v, preferred_element_type=f32)
    den = w_inter * jnp.sum(q.astype(f32) * n_prev, axis=1, keepdims=True)
    den = den + jnp.sum(s, axis=1, keepdims=True)
    h = num / jnp.maximum(jnp.abs(den), jnp.exp(-m_t))
    b_last = b_row[:, L - 1:L]
    dec_row = b_last - b_row + ig_row
    m_new = jnp.maximum(b_last + m_prev, jnp.max(dec_row, axis=1, keepdims=True))
    wk_row = jnp.exp(dec_row - m_new)
    sc = jnp.exp(b_last + m_prev - m_new)
    wk_col = jnp.exp(b_last - b_col + ig_col - m_new)
    kw = (k.astype(f32) * wk_col).astype(bf16)
    c_new = sc * c_prev + lax.dot_general(kw, v, (((0,), (0,)), ((), ())),
                                          preferred_element_type=f32)
    wk8 = jnp.broadcast_to(wk_row, (SUBLANE, L)).astype(bf16)
    n_new = sc * n_prev + jnp.dot(wk8, k, preferred_element_type=f32)[0:1, :]
    return h, c_new, n_new, m_new


def _mlstm_prompt_kernel(qkv_q, qkv_k, qkv_v, sm_ref, poz_ref, bif_ref, g_ref,
                         hm_ref, c_out, nm_out, c_s, n_s, m_s):
    ci = pl.program_id(1)
    L = qkv_q.shape[0]

    @pl.when(ci == 0)
    def _():
        c_s[...] = jnp.zeros_like(c_s)
        n_s[...] = jnp.zeros_like(n_s)
        m_s[...] = jnp.zeros_like(m_s)

    r = lax.broadcasted_iota(i32, (L, L), 0)
    c = lax.broadcasted_iota(i32, (L, L), 1)
    tri = c <= r
    eye = c == r
    sm = sm_ref[...] + bif_ref[...]
    for h in range(M_HEADS):
        ig_col = sm[:, SM_I + h:SM_I + h + 1]
        lf_col = _log_sigmoid(sm[:, SM_F + h:SM_F + h + 1])
        sl = slice(h * M_DIM, (h + 1) * M_DIM)
        hh, c_new, n_new, m_new = _mlstm_head(
            qkv_q[:, sl], qkv_k[:, sl], qkv_v[:, sl], ig_col, lf_col,
            c_s[h], n_s[h][0:1, :], m_s[h][0:1, 0:1], tri, eye)
        c_s[h] = c_new
        n_s[h] = jnp.broadcast_to(n_new, (SUBLANE, M_DIM))
        m_s[h] = jnp.broadcast_to(m_new, (SUBLANE, LANE))
        ms = jnp.mean(hh * hh, axis=-1, keepdims=True)
        hn = hh * lax.rsqrt(ms + EPS) * g_ref[:, sl]
        hm_ref[:, sl] = (hn * poz_ref[:, sl]).astype(bf16)

    @pl.when(ci == pl.num_programs(1) - 1)
    def _():
        c_out[0] = c_s[...]
        for h in range(M_HEADS):
            nm_out[0, h] = jnp.concatenate(
                [n_s[h][0:1, :], jnp.broadcast_to(m_s[h][0:1, 0:1], (SUBLANE - 1, M_DIM))], axis=0)


def _mlstm_prompt(qkv, small, poz, bif_lane, g_mlstm, bsz, seq, L):
    nc = seq // L
    m = bsz * seq
    blk = lambda off: pl.BlockSpec((L, M_WIDTH), lambda b, c, off=off: (b * nc + c, off))
    return pl.pallas_call(
        _mlstm_prompt_kernel,
        out_shape=[jax.ShapeDtypeStruct((m, M_WIDTH), bf16),
                   jax.ShapeDtypeStruct((bsz, M_HEADS, M_DIM, M_DIM), f32),
                   jax.ShapeDtypeStruct((bsz, M_HEADS, SUBLANE, M_DIM), f32)],
        grid=(bsz, nc),
        in_specs=[blk(0), blk(1), blk(2),
                  pl.BlockSpec((L, LANE), lambda b, c: (b * nc + c, 0)),
                  pl.BlockSpec((L, M_WIDTH), lambda b, c: (b * nc + c, 0)),
                  pl.BlockSpec((1, LANE), lambda b, c: (0, 0)),
                  pl.BlockSpec((1, M_WIDTH), lambda b, c: (0, 0))],
        out_specs=[pl.BlockSpec((L, M_WIDTH), lambda b, c: (b * nc + c, 0)),
                   pl.BlockSpec((1, M_HEADS, M_DIM, M_DIM), lambda b, c: (b, 0, 0, 0)),
                   pl.BlockSpec((1, M_HEADS, SUBLANE, M_DIM), lambda b, c: (b, 0, 0, 0))],
        scratch_shapes=[pltpu.VMEM((M_HEADS, M_DIM, M_DIM), f32),
                        pltpu.VMEM((M_HEADS, SUBLANE, M_DIM), f32),
                        pltpu.VMEM((M_HEADS, SUBLANE, LANE), f32)],
        compiler_params=_cparams(("parallel", "arbitrary")),
        name="mlstm_prompt",
    )(qkv, qkv, qkv, small, poz, bif_lane, g_mlstm.reshape(1, M_WIDTH))


SAMPLE_PAD = 128


def _mlstm_sample_kernel(q_ref, k_ref, v_ref, sm_ref, poz_ref, bif_ref, g_ref, c_in, n_in, m_in,
                         hm_ref, c_out, n_out, m_out, *, tokens):
    L = SAMPLE_PAD
    rows = q_ref.shape[0]
    reqs = rows // tokens
    r = lax.broadcasted_iota(i32, (L, L), 0)
    c = lax.broadcasted_iota(i32, (L, L), 1)
    tri = c <= r
    eye = c == r
    row = lax.broadcasted_iota(i32, (L, 1), 0)
    row8 = lax.broadcasted_iota(i32, (rows, 1), 0)

    def pad(a):
        return jnp.concatenate([a, jnp.zeros((L - rows, a.shape[1]), a.dtype)], axis=0)

    sm = pad(sm_ref[...] + bif_ref[...])
    qp = pad(q_ref[...])
    kp = pad(k_ref[...])
    vp = pad(v_ref[...])
    for h in range(M_HEADS):
        sl = slice(h * M_DIM, (h + 1) * M_DIM)
        q = qp[:, sl].astype(bf16)
        k = kp[:, sl].astype(bf16)
        v = vp[:, sl].astype(bf16)
        h_tile = jnp.zeros((rows, M_DIM), f32)
        for rr in range(reqs):
            mine = (row >= rr * tokens) & (row < (rr + 1) * tokens)
            ig_col = jnp.where(mine, sm[:, SM_I + h:SM_I + h + 1], NEG_INF)
            lf_col = jnp.where(mine, _log_sigmoid(sm[:, SM_F + h:SM_F + h + 1]), 0.0)
            hh, c_new, n_new, m_new = _mlstm_head(
                q, k, v, ig_col, lf_col, c_in[rr, h], n_in[rr, h], m_in[rr, h], tri, eye)
            c_out[rr, h] = c_new
            n_out[rr, h] = n_new
            m_out[rr, h] = m_new
            mine8 = (row8 >= rr * tokens) & (row8 < (rr + 1) * tokens)
            h_tile = jnp.where(mine8, hh[0:rows, :], h_tile)
        ms = jnp.mean(h_tile * h_tile, axis=-1, keepdims=True)
        hn = h_tile * lax.rsqrt(ms + EPS) * g_ref[:, sl]
        hm_ref[:, sl] = hn * poz_ref[:, sl]


def _mlstm_sample(qkv, small, poz, bif_lane, g_mlstm, state_c, state_n, state_m, tokens):
    m = qkv.shape[0]
    rows = SUBLANE
    reqs = rows // tokens
    nreq = m // tokens
    blk = lambda off: pl.BlockSpec((rows, M_WIDTH), lambda i, off=off: (i, off))
    n4 = state_n.reshape(nreq, M_HEADS, 1, M_DIM)
    m4 = state_m.reshape(nreq, M_HEADS, 1, 1)
    return pl.pallas_call(
        functools.partial(_mlstm_sample_kernel, tokens=tokens),
        out_shape=[jax.ShapeDtypeStruct((m, M_WIDTH), f32),
                   jax.ShapeDtypeStruct(state_c.shape, f32),
                   jax.ShapeDtypeStruct(n4.shape, f32),
                   jax.ShapeDtypeStruct(m4.shape, f32)],
        grid=(m // rows,),
        in_specs=[blk(0), blk(1), blk(2),
                  pl.BlockSpec((rows, LANE), lambda i: (i, 0)),
                  pl.BlockSpec((rows, M_WIDTH), lambda i: (i, 0)),
                  pl.BlockSpec((1, LANE), lambda i: (0, 0)),
                  pl.BlockSpec((1, M_WIDTH), lambda i: (0, 0)),
                  pl.BlockSpec((reqs, M_HEADS, M_DIM, M_DIM), lambda i: (i, 0, 0, 0)),
                  pl.BlockSpec((reqs, M_HEADS, 1, M_DIM), lambda i: (i, 0, 0, 0)),
                  pl.BlockSpec((reqs, M_HEADS, 1, 1), lambda i: (i, 0, 0, 0))],
        out_specs=[pl.BlockSpec((rows, M_WIDTH), lambda i: (i, 0)),
                   pl.BlockSpec((reqs, M_HEADS, M_DIM, M_DIM), lambda i: (i, 0, 0, 0)),
                   pl.BlockSpec((reqs, M_HEADS, 1, M_DIM), lambda i: (i, 0, 0, 0)),
                   pl.BlockSpec((reqs, M_HEADS, 1, 1), lambda i: (i, 0, 0, 0))],
        compiler_params=_cparams(("parallel",)),
        name="mlstm_sample",
    )(qkv, qkv, qkv, small, poz, bif_lane, g_mlstm.reshape(1, M_WIDTH), state_c, n4, m4)


def _kth_key(count_ge, kk, shape):
    def step(_, carry):
        lo, hi = carry
        mid = _floor_avg(lo, hi)
        ge = count_ge(mid) >= kk
        return jnp.where(ge, mid, lo), jnp.where(ge, hi, mid)

    lo, _ = lax.fori_loop(0, 32, step, (jnp.full(shape, INT_MIN, i32), jnp.full(shape, INT_MAX, i32)))
    return lo


def _dsa_prompt_kernel(k_ref, vt_ref, ik_ref, qt_ref, iqt_ref, smt_ref, bias_ref, az_ref,
                       o_ref, key_s, acc_s, m_s, l_s, *, topk):
    qi = pl.program_id(1)
    nblk = qi + 1
    TQ = LANE
    TK = LANE
    t_abs = qi * TQ + lax.broadcasted_iota(i32, (1, TQ), 1)
    s_loc = lax.broadcasted_iota(i32, (TK, 1), 0)
    iqt = iqt_ref[0]
    w_rows = smt_ref[0][SM_W:SM_W + SUBLANE, :]

    def score_blk(j, _):
        ikb = ik_ref[pl.ds(j * TK, TK), :][:, SM_IK:SM_IK + IDX_DIM].astype(bf16)
        acc = jnp.zeros((TK, TQ), f32)
        for h in range(IDX_HEADS):
            s = jnp.dot(ikb, iqt[h * IDX_DIM:(h + 1) * IDX_DIM, :], preferred_element_type=f32)
            acc = acc + jnp.maximum(s, 0.0) * w_rows[h:h + 1, :]
        acc = jnp.where(j * TK + s_loc <= t_abs, acc, NEG_INF)
        key_s[pl.ds(j * TK, TK), :] = _float_key(acc)
        return 0

    lax.fori_loop(0, nblk, score_blk, 0)

    def count_ge(th):
        def body(j, acc):
            return acc + jnp.where(key_s[pl.ds(j * TK, TK), :] >= th, 1.0, 0.0)
        acc = lax.fori_loop(0, nblk, body, jnp.zeros((TK, TQ), f32))
        return jnp.sum(acc, axis=0, keepdims=True)

    nvalid = (t_abs + 1).astype(f32)
    kk = jnp.minimum(float(topk), nvalid)
    th = _kth_key(count_ge, kk, (1, TQ))
    th = jnp.where(nvalid <= float(topk), KEY_NEG_INF + 1, th)

    tie = (nvalid > float(topk)) & (count_ge(th) > kk)
    any_tie = jnp.max(jnp.where(tie, 1.0, 0.0))

    @pl.when(any_tie > 0.0)
    def _():
        need = jnp.where(tie, kk - count_ge(th + 1), float(INT_MAX))
        rr = lax.broadcasted_iota(i32, (TK, TK), 0)
        cc = lax.broadcasted_iota(i32, (TK, TK), 1)
        lower = jnp.where(cc < rr, 1.0, 0.0).astype(bf16)

        def body(j, run):
            kb = key_s[pl.ds(j * TK, TK), :]
            eq = kb == th
            eqf = jnp.where(eq, 1.0, 0.0)
            rank = run + jnp.dot(lower, eqf.astype(bf16), preferred_element_type=f32)
            key_s[pl.ds(j * TK, TK), :] = jnp.where(eq & (rank >= need), KEY_UNSELECTED, kb)
            return run + jnp.sum(eqf, axis=0, keepdims=True)

        lax.fori_loop(0, nblk, body, jnp.zeros((1, TQ), f32))

    m_s[...] = jnp.full(m_s.shape, NEG_INF, f32)
    l_s[...] = jnp.zeros_like(l_s)
    acc_s[...] = jnp.zeros_like(acc_s)
    qt = qt_ref[0]

    def att_blk(j, _):
        sel = key_s[pl.ds(j * TK, TK), :] >= th
        kb = k_ref[pl.ds(j * TK, TK), :]
        vtb = vt_ref[j]
        d = jnp.minimum(qi - j, 2)
        for h in range(A_HEADS):
            kv = h // A_GROUP
            lg = jnp.dot(kb[:, kv * A_DIM:(kv + 1) * A_DIM], qt[h * A_DIM:(h + 1) * A_DIM, :],
                         preferred_element_type=f32)
            lg = jnp.where(sel, lg + bias_ref[d, h], NEG_INF)
            m_old = m_s[h][0:1, :]
            m_new = jnp.maximum(m_old, jnp.max(lg, axis=0, keepdims=True))
            m_safe = jnp.where(m_new == NEG_INF, 0.0, m_new)
            p = jnp.exp(lg - m_safe)
            alpha = jnp.exp(m_old - m_safe)
            l_new = alpha * l_s[h][0:1, :] + jnp.sum(p, axis=0, keepdims=True)
            acc_s[h] = alpha * acc_s[h] + jnp.dot(vtb[kv * A_DIM:(kv + 1) * A_DIM, :], p.astype(bf16),
                                                  preferred_element_type=f32)
            m_s[h] = jnp.broadcast_to(m_new, (SUBLANE, TQ))
            l_s[h] = jnp.broadcast_to(l_new, (SUBLANE, TQ))
        return 0

    lax.fori_loop(0, nblk, att_blk, 0)

    for h in range(A_HEADS):
        sl = slice(h * A_DIM, (h + 1) * A_DIM)
        ot = acc_s[h] / l_s[h][0:1, :]
        o_ref[:, sl] = (ot.T * az_ref[:, sl]).astype(o_ref.dtype)


def _dsa_prompt(k_bf, vt, small, qt, iqt, smallt, bias_tiles, az, bsz, seq, topk):
    nq = seq // LANE
    m = bsz * seq
    return pl.pallas_call(
        functools.partial(_dsa_prompt_kernel, topk=topk),
        out_shape=jax.ShapeDtypeStruct((m, A_WIDTH), bf16),
        grid=(bsz, nq),
        in_specs=[pl.BlockSpec((seq, A_KV_WIDTH), lambda b, i: (b, 0)),
                  pl.BlockSpec((nq, A_KV_WIDTH, LANE), lambda b, i: (b, 0, 0)),
                  pl.BlockSpec((seq, LANE), lambda b, i: (b, 0)),
                  pl.BlockSpec((1, A_WIDTH, LANE), lambda b, i: (b * nq + i, 0, 0)),
                  pl.BlockSpec((1, IDX_HEADS * IDX_DIM, LANE), lambda b, i: (b * nq + i, 0, 0)),
                  pl.BlockSpec((1, LANE, LANE), lambda b, i: (b * nq + i, 0, 0)),
                  pl.BlockSpec((3, A_HEADS, LANE, LANE), lambda b, i: (0, 0, 0, 0)),
                  pl.BlockSpec((LANE, A_WIDTH), lambda b, i: (b * nq + i, 0))],
        out_specs=pl.BlockSpec((LANE, A_WIDTH), lambda b, i: (b * nq + i, 0)),
        scratch_shapes=[pltpu.VMEM((seq, LANE), i32),
                        pltpu.VMEM((A_HEADS, A_DIM, LANE), f32),
                        pltpu.VMEM((A_HEADS, SUBLANE, LANE), f32),
                        pltpu.VMEM((A_HEADS, SUBLANE, LANE), f32)],
        compiler_params=_cparams(("parallel", "arbitrary")),
        name="dsa_prompt",
    )(k_bf, vt, small, qt, iqt, smallt, bias_tiles, az)


KV_CHUNK_PAGES = 8


def _ds_scores_kernel(pt_ref, iq_ref, sm_ref, cache_ref, o_ref, ikbuf, sem, *, tokens, npages):
    pair = pl.program_id(0)
    rows = iq_ref.shape[0]
    reqs = rows // tokens
    sm = sm_ref[...]
    iq = iq_ref[...]
    iq_heads = jnp.concatenate([iq[:, h * IDX_DIM:(h + 1) * IDX_DIM] for h in range(IDX_HEADS)],
                               axis=0).astype(bf16)
    row = lax.broadcasted_iota(i32, (rows, 1), 0)

    def page_copy(rr, p):
        page = pt_ref[(pair * reqs + rr) * npages + p]
        return pltpu.make_async_copy(cache_ref.at[page], ikbuf.at[rr, p], sem.at[rr])

    def head_sum(s):
        acc = jnp.zeros((rows, s.shape[1]), f32)
        for h in range(IDX_HEADS):
            acc = acc + jnp.maximum(s[h * rows:(h + 1) * rows, :], 0.0) * sm[:, SM_W + h:SM_W + h + 1]
        return acc

    for rr in range(reqs):
        def issue(p, _, rr=rr):
            page_copy(rr, p).start()
            return 0
        lax.fori_loop(0, npages, issue, 0)

    for rr in range(reqs):
        def drain(p, _, rr=rr):
            page_copy(rr, p).wait()
            return 0
        lax.fori_loop(0, npages, drain, 0)
        mine = (row >= rr * tokens) & (row < (rr + 1) * tokens)

        def page_scores(p, _, rr=rr, mine=mine):
            s = jnp.dot(iq_heads, ikbuf[rr, p].astype(bf16), preferred_element_type=f32)
            sc = head_sum(s)
            if rr == 0:
                o_ref[0, p] = sc
            else:
                o_ref[0, p] = jnp.where(mine, sc, o_ref[0, p])
            return 0
        lax.fori_loop(0, npages, page_scores, 0)

    ik_new = jnp.concatenate([sm[:, SM_IK:SM_IK + IDX_DIM],
                              jnp.zeros((LANE - rows, IDX_DIM), f32)], axis=0).astype(bf16)
    s = lax.dot_general(iq_heads, ik_new, (((1,), (1,)), ((), ())), preferred_element_type=f32)
    col = lax.broadcasted_iota(i32, (1, LANE), 1)
    ok = (col < rows) & ((col // tokens) == (row // tokens)) & (col <= row)
    o_ref[0, npages] = jnp.where(ok, head_sum(s), NEG_INF)


def _ds_scores(page_table, iq, small, cache_ik_t, tokens):
    m = iq.shape[0]
    nreq, npages = page_table.shape
    rows = SUBLANE
    reqs = rows // tokens
    return pl.pallas_call(
        functools.partial(_ds_scores_kernel, tokens=tokens, npages=npages),
        out_shape=jax.ShapeDtypeStruct((m // rows, npages + 1, rows, LANE), f32),
        grid_spec=pltpu.PrefetchScalarGridSpec(
            num_scalar_prefetch=1,
            grid=(m // rows,),
            in_specs=[pl.BlockSpec((rows, IDX_HEADS * IDX_DIM), lambda i, pt: (i, 0)),
                      pl.BlockSpec((rows, LANE), lambda i, pt: (i, 0)),
                      pl.BlockSpec(memory_space=pl.ANY)],
            out_specs=pl.BlockSpec((1, npages + 1, rows, LANE), lambda i, pt: (i, 0, 0, 0)),
            scratch_shapes=[pltpu.VMEM((reqs, npages, IDX_DIM, PAGE_SIZE), f32),
                            pltpu.SemaphoreType.DMA((reqs,))]),
        compiler_params=_cparams(("arbitrary",)),
        name="dsa_sample_scores",
    )(page_table.reshape(-1), iq, small, cache_ik_t)


def _ds_mask_kernel(sc_ref, o_ref, key_s, *, topk):
    tb, p1 = sc_ref.shape[0], sc_ref.shape[1]
    key_s[...] = _float_key(sc_ref[...])

    def count_ge(th):
        ind = jnp.where(key_s[...] >= th, 1.0, 0.0)
        return jnp.sum(jnp.sum(ind, axis=1, keepdims=True), axis=3, keepdims=True)

    shape = (tb, 1, SUBLANE, 1)
    nvalid = count_ge(jnp.full(shape, KEY_NEG_INF + 1, i32))
    kk = jnp.minimum(float(topk), nvalid)
    th = _kth_key(count_ge, kk, shape)
    th = jnp.where(nvalid <= float(topk), KEY_NEG_INF + 1, th)
    tie = (nvalid > float(topk)) & (count_ge(th) > kk)
    any_tie = jnp.max(jnp.where(tie, 1.0, 0.0))

    @pl.when(any_tie > 0.0)
    def _():
        need = jnp.where(tie, kk - count_ge(th + 1), float(INT_MAX))
        rr = lax.broadcasted_iota(i32, (LANE, LANE), 0)
        cc = lax.broadcasted_iota(i32, (LANE, LANE), 1)
        upper = jnp.where(rr < cc, 1.0, 0.0).astype(bf16)

        def page(p, run):
            new = []
            for t in range(tb):
                kb = key_s[t, p]
                eq = kb == th[t, 0]
                eqf = jnp.where(eq, 1.0, 0.0)
                rank = run[t] + jnp.dot(eqf.astype(bf16), upper, preferred_element_type=f32)
                key_s[t, p] = jnp.where(eq & (rank >= need[t, 0]), KEY_UNSELECTED, kb)
                new.append(run[t] + jnp.sum(eqf, axis=1, keepdims=True))
            return tuple(new)

        lax.fori_loop(0, p1, page, tuple(jnp.zeros((SUBLANE, 1), f32) for _ in range(tb)))

    o_ref[...] = jnp.where(key_s[...] >= th, 0.0, NEG_INF)


def _ds_mask(scores, topk, tb):
    nt, p1, rows, lanes = scores.shape
    blk = pl.BlockSpec((tb, p1, rows, lanes), lambda i: (i, 0, 0, 0))
    return pl.pallas_call(
        functools.partial(_ds_mask_kernel, topk=topk),
        out_shape=jax.ShapeDtypeStruct(scores.shape, f32),
        grid=(nt // tb,),
        in_specs=[blk],
        out_specs=blk,
        scratch_shapes=[pltpu.VMEM((tb, p1, rows, lanes), i32)],
        compiler_params=_cparams(("parallel",)),
        name="dsa_sample_mask",
    )(scores)


def _ds_attend_kernel(pt_ref, q_ref, knew_ref, vnew_ref, mask_ref, biasl_ref, biasn_ref, az_ref,
                      kc_ref, vc_ref, o_ref, kbuf, vbuf, sem, *, tokens, npages):
    pair = pl.program_id(0)
    rows = q_ref.shape[0]
    reqs = rows // tokens
    nchunk = npages // KV_CHUNK_PAGES
    total = reqs * nchunk
    row = lax.broadcasted_iota(i32, (rows, 1), 0)
    lo_half = row < tokens

    def half_swap(a):
        return pltpu.roll(a, tokens, axis=0)

    def own_rows(a, rr):
        return jnp.where(lo_half, a, half_swap(a)) if rr == 0 else jnp.where(lo_half, half_swap(a), a)

    def chunk_copies(g, slot):
        cps = []
        for p in range(KV_CHUNK_PAGES):
            page = pt_ref[pair * reqs * npages + g * KV_CHUNK_PAGES + p]
            dst = pl.ds(p * PAGE_SIZE, PAGE_SIZE)
            cps.append(pltpu.make_async_copy(kc_ref.at[page], kbuf.at[slot, dst], sem.at[slot, 0]))
            cps.append(pltpu.make_async_copy(vc_ref.at[page], vbuf.at[slot, dst], sem.at[slot, 1]))
        return cps

    for cp in chunk_copies(0, 0):
        cp.start()

    def update(state, lg, vb):
        m_old, l_old, acc = state
        m_new = jnp.maximum(m_old, jnp.max(lg, axis=1, keepdims=True))
        m_safe = jnp.where(m_new == NEG_INF, 0.0, m_new)
        p = jnp.exp(lg - m_safe)
        alpha = jnp.exp(m_old - m_safe)
        l_new = alpha * l_old + jnp.sum(p, axis=1, keepdims=True)
        acc = alpha * acc + jnp.dot(p.astype(bf16), vb, preferred_element_type=f32)
        return m_new, l_new, acc

    def pad_rows(a):
        return jnp.concatenate([a, jnp.zeros((LANE - a.shape[0], a.shape[1]), a.dtype)], axis=0).astype(bf16)

    q = q_ref[...]
    out = [jnp.zeros((rows, A_DIM), f32) for _ in range(A_HEADS)]
    for rr in range(reqs):
        qk = []
        for kv in range(A_KV_HEADS):
            a = q[:, (A_GROUP * kv) * A_DIM:(A_GROUP * kv + 1) * A_DIM]
            b = q[:, (A_GROUP * kv + 1) * A_DIM:(A_GROUP * kv + 2) * A_DIM]
            qkv = jnp.where(lo_half, a, half_swap(b)) if rr == 0 else jnp.where(lo_half, half_swap(a), b)
            qk.append(qkv.astype(bf16))

        def body(c, state, rr=rr, qk=qk):
            g = rr * nchunk + c
            slot = g % 2

            @pl.when(g + 1 < total)
            def _():
                for cp in chunk_copies(g + 1, 1 - slot):
                    cp.start()

            for cp in chunk_copies(g, slot):
                cp.wait()
            mtile = jnp.concatenate([mask_ref[0, c * KV_CHUNK_PAGES + p] for p in range(KV_CHUNK_PAGES)],
                                    axis=1)
            m8 = own_rows(mtile, rr)
            last = jnp.where(c == nchunk - 1, 1.0, 0.0)
            new_state = []
            for kv in range(A_KV_HEADS):
                kb = kbuf[slot, :, kv, :].astype(bf16)
                vb = vbuf[slot, :, kv, :].astype(bf16)
                lg = lax.dot_general(qk[kv], kb, (((1,), (1,)), ((), ())), preferred_element_type=f32)
                lg = lg + last * biasl_ref[kv] + m8
                new_state.append(update(state[kv], lg, vb))
            return tuple(new_state)

        state0 = tuple((jnp.full((rows, 1), NEG_INF, f32), jnp.zeros((rows, 1), f32),
                        jnp.zeros((rows, A_DIM), f32)) for _ in range(A_KV_HEADS))
        state = lax.fori_loop(0, nchunk, body, state0)

        m8n = own_rows(mask_ref[0, npages], rr)
        for kv in range(A_KV_HEADS):
            sl = slice(kv * A_DIM, (kv + 1) * A_DIM)
            lg = lax.dot_general(qk[kv], pad_rows(knew_ref[:, sl]), (((1,), (1,)), ((), ())),
                                 preferred_element_type=f32)
            lg = lg + biasn_ref[kv] + m8n
            _, l_fin, acc = update(state[kv], lg, pad_rows(vnew_ref[:, sl]))
            o_kv = acc / l_fin
            h0, h1 = A_GROUP * kv, A_GROUP * kv + 1
            if rr == 0:
                out[h0] = jnp.where(lo_half, o_kv, out[h0])
                out[h1] = jnp.where(lo_half, half_swap(o_kv), out[h1])
            else:
                out[h0] = jnp.where(lo_half, out[h0], half_swap(o_kv))
                out[h1] = jnp.where(lo_half, out[h1], o_kv)

    for h in range(A_HEADS):
        sl = slice(h * A_DIM, (h + 1) * A_DIM)
        o_ref[:, sl] = out[h] * az_ref[:, sl]


def _ds_attend(page_table, q, k_new, v_new, mask, bias_last, bias_new, az, cache_k, cache_v, tokens):
    nreq, npages = page_table.shape
    m = q.shape[0]
    rows = SUBLANE
    chunk = KV_CHUNK_PAGES * PAGE_SIZE
    tile = lambda w: pl.BlockSpec((rows, w), lambda i, pt: (i, 0))
    const3 = lambda a: pl.BlockSpec(a.shape, lambda i, pt: (0, 0, 0))
    return pl.pallas_call(
        functools.partial(_ds_attend_kernel, tokens=tokens, npages=npages),
        out_shape=jax.ShapeDtypeStruct((m, A_WIDTH), f32),
        grid_spec=pltpu.PrefetchScalarGridSpec(
            num_scalar_prefetch=1,
            grid=(m // rows,),
            in_specs=[tile(A_WIDTH), tile(A_KV_WIDTH), tile(A_KV_WIDTH),
                      pl.BlockSpec((1, npages + 1, rows, LANE), lambda i, pt: (i, 0, 0, 0)),
                      const3(bias_last), const3(bias_new), tile(A_WIDTH),
                      pl.BlockSpec(memory_space=pl.ANY),
                      pl.BlockSpec(memory_space=pl.ANY)],
            out_specs=tile(A_WIDTH),
            scratch_shapes=[pltpu.VMEM((2, chunk, A_KV_HEADS, A_DIM), f32),
                            pltpu.VMEM((2, chunk, A_KV_HEADS, A_DIM), f32),
                            pltpu.SemaphoreType.DMA((2, 2))]),
        compiler_params=_cparams(("arbitrary",)),
        name="dsa_sample_attend",
    )(page_table.reshape(-1), q, k_new, v_new, mask, bias_last, bias_new, az, cache_k, cache_v)


def _final_kernel(x_ref, hm_ref, ha_ref, sgm_ref, sga_ref, wbm_ref, wba_ref, wo_ref, o_ref):
    ym = jnp.dot(hm_ref[...].astype(bf16), wbm_ref[...], preferred_element_type=f32)
    ya = jnp.dot(ha_ref[...].astype(bf16), wba_ref[...], preferred_element_type=f32)
    merged = sgm_ref[...] * ym + sga_ref[...] * ya
    o_ref[...] = x_ref[...] + jnp.dot(merged.astype(bf16), wo_ref[...], preferred_element_type=f32)


def _final(x, hm, ha, sg, wbm, wba, wo, tm):
    m, d = x.shape
    row = lambda w: pl.BlockSpec((tm, w), lambda i: (i, 0))
    full = lambda a: pl.BlockSpec(a.shape, lambda i: (0, 0))
    return pl.pallas_call(
        _final_kernel,
        out_shape=jax.ShapeDtypeStruct((m, d), f32),
        grid=(m // tm,),
        in_specs=[row(d), row(M_WIDTH), row(A_WIDTH),
                  pl.BlockSpec((tm, d), lambda i: (i, 0)), pl.BlockSpec((tm, d), lambda i: (i, 1)),
                  full(wbm), full(wba), full(wo)],
        out_specs=row(d),
        compiler_params=_cparams(("parallel",)),
        name="merge_out",
    )(x, hm, ha, sg, sg, wbm, wba, wo)


def _segments(w_in):
    widths = (("m_q", M_WIDTH), ("m_k", M_WIDTH), ("m_v", M_WIDTH), ("m_o", M_WIDTH), ("m_z", M_WIDTH),
              ("m_i", M_HEADS), ("m_f", M_HEADS),
              ("a_q", A_WIDTH), ("a_k", A_KV_WIDTH), ("a_v", A_KV_WIDTH), ("a_z", A_WIDTH),
              ("ix_q", IDX_HEADS * IDX_DIM), ("ix_k", IDX_DIM), ("ix_w", IDX_HEADS),
              ("g_m", w_in.shape[0]), ("g_a", w_in.shape[0]))
    seg, off = {}, 0
    for name, w in widths:
        seg[name] = w_in[:, off:off + w]
        off += w
    assert off == w_in.shape[1]
    return seg


def _rel_bucket(dist):
    max_exact = REL_BUCKETS // 2
    d = jnp.maximum(dist.astype(f32), 1.0)
    large = max_exact + (jnp.log(d / max_exact) / math.log(REL_MAX_DIST / max_exact)
                         * (REL_BUCKETS - max_exact)).astype(i32)
    large = jnp.minimum(large, REL_BUCKETS - 1)
    return jnp.where(dist < max_exact, dist, large)


def _bias_of(dist, rel_bias):
    far = rel_bias[REL_BUCKETS - 1]
    return rel_bias[_rel_bucket(jnp.maximum(dist, 0))] - far


def kernel(x_prompt, x_sample, cache_k, cache_v, cache_idx_k, state_C, state_n, state_m, page_table,
           rel_bias, w_norm, w_in, b_if, g_mlstm, g_q, g_k, w_branch_m, w_branch_a, w_out):
    assert w_in.shape[0] == 1, "single-layer trunk"
    bsz, seq, d = x_prompt.shape
    nreq, tokens, _ = x_sample.shape
    npages = page_table.shape[1]
    past = npages * PAGE_SIZE
    assert seq % LANE == 0 and 2 * tokens == SUBLANE and (nreq * tokens) % SUBLANE == 0
    assert npages % KV_CHUNK_PAGES == 0 and A_GROUP == 2
    assert REL_MAX_DIST <= LANE + 1

    rel_bias = rel_bias.astype(f32)
    seg = _segments(w_in[0].astype(f32))
    cast = lambda a: a.astype(bf16)
    w_qkv = cast(jnp.concatenate([seg["m_q"], seg["m_k"] * (M_DIM ** -0.5), seg["m_v"]], axis=1))
    w_small = cast(jnp.concatenate(
        [seg["ix_k"], seg["m_i"], seg["m_f"], seg["ix_w"],
         jnp.zeros((d, LANE - IDX_DIM - 2 * M_HEADS - IDX_HEADS), f32)], axis=1))
    w_g = cast(jnp.concatenate([seg["g_m"], seg["g_a"]], axis=1))
    w_o, w_z, w_az = cast(seg["m_o"]), cast(seg["m_z"]), cast(seg["a_z"])
    w_aq, w_ak, w_av, w_iq = cast(seg["a_q"]), cast(seg["a_k"]), cast(seg["a_v"]), cast(seg["ix_q"])
    gq_scaled = g_q[0].astype(f32) * (A_DIM ** -0.5)
    gk = g_k[0].astype(f32).reshape(1, A_DIM)
    bif_lane = jnp.zeros((1, LANE), f32).at[0, SM_I:SM_I + 2 * M_HEADS].set(b_if[0].astype(f32))
    wbm, wba, wo = cast(w_branch_m[0]), cast(w_branch_a[0]), cast(w_out[0])

    mp = bsz * seq
    tm = min(1024, mp)
    tn = 512
    xp = x_prompt.reshape(mp, d)
    xn = _rmsnorm(xp, w_norm[0], min(512, mp))
    (qkv,) = _proj_nat(_proj_cast_kernel, xn, [w_qkv], [], [bf16], tm, tn, "proj_mqkv")
    (small,) = _proj_nat(functools.partial(_proj_act_kernel, act=None), xn, [w_small], [], [f32],
                         tm, LANE, "proj_small")
    (poz,) = _proj_nat(_proj_oz_kernel, xn, [w_o, w_z], [], [f32], tm, tn, "proj_moz")
    k_rows, k_bf = _proj_nat(_proj_knorm_kernel, xn, [w_ak], [gk], [f32, bf16], tm, tn, "proj_ak")
    (v_rows,) = _proj_nat(functools.partial(_proj_act_kernel, act=None), xn, [w_av], [], [f32],
                          tm, tn, "proj_av")
    (az,) = _proj_nat(functools.partial(_proj_act_kernel, act="silu"), xn, [w_az], [], [f32],
                      tm, tn, "proj_az")
    (sg,) = _proj_nat(functools.partial(_proj_act_kernel, act="sigmoid"), xn, [w_g], [], [f32],
                      tm, tn, "proj_gates")
    tmt = min(512, mp)
    gq_tile = jnp.broadcast_to(jnp.tile(gq_scaled, A_HEADS)[:, None], (A_WIDTH, LANE))
    qt = _proj_T(xn, w_aq.T, gq_tile, "qnorm", bf16, tmt, 512, "projT_aq")
    vt = _proj_T(xn, w_av.T, None, "cast", bf16, tmt, 512, "projT_av")
    iqt = _proj_T(xn, w_iq.T, None, "cast", bf16, tmt, IDX_HEADS * IDX_DIM, "projT_iq")
    smallt = _proj_T(xn, w_small.T, None, "cast", f32, tmt, LANE, "projT_small")

    chunk = 256 if seq % 256 == 0 else LANE
    hm, c_p, nm_p = _mlstm_prompt(qkv, small, poz, bif_lane, g_mlstm[0].astype(f32), bsz, seq, chunk)

    tl = jnp.arange(LANE)
    dist0 = tl[None, :] - tl[:, None]
    bias_tiles = jnp.stack([_bias_of(dist0, rel_bias), _bias_of(dist0 + LANE, rel_bias),
                            jnp.zeros((LANE, LANE, A_HEADS), f32)])
    bias_tiles = jnp.transpose(bias_tiles, (0, 3, 1, 2))
    topk_p = min(TOPK_MAX, seq // 4)
    ha = _dsa_prompt(k_bf, vt, small, qt, iqt, smallt, bias_tiles, az, bsz, seq, topk_p)
    y_p = _final(xp, hm, ha, sg, wbm, wba, wo, min(512, mp))

    ms = nreq * tokens
    xs = x_sample.reshape(ms, d)
    tms = min(512, ms)
    xns = _rmsnorm(xs, w_norm[0], tms)
    nat = lambda body, ws, ex, outs, name, tn_=tn: _proj_nat(body, xns, ws, ex, outs, tms, tn_, name)
    (qkv_s,) = nat(_proj_cast_kernel, [w_qkv], [], [f32], "sproj_mqkv")
    (small_s,) = nat(functools.partial(_proj_act_kernel, act=None), [w_small], [], [f32], "sproj_small", LANE)
    (poz_s,) = nat(_proj_oz_kernel, [w_o, w_z], [], [f32], "sproj_moz")
    (q_s,) = nat(_proj_qnorm_kernel, [w_aq], [gq_scaled.reshape(1, A_DIM)], [f32], "sproj_aq")
    k_rows_s, _ = nat(_proj_knorm_kernel, [w_ak], [gk], [f32, bf16], "sproj_ak")
    (v_rows_s,) = nat(functools.partial(_proj_act_kernel, act=None), [w_av], [], [f32], "sproj_av")
    (az_s,) = nat(functools.partial(_proj_act_kernel, act="silu"), [w_az], [], [f32], "sproj_az")
    (sg_s,) = nat(functools.partial(_proj_act_kernel, act="sigmoid"), [w_g], [], [f32], "sproj_gates")
    (iq_s,) = nat(functools.partial(_proj_act_kernel, act=None), [w_iq], [], [f32], "sproj_iq",
                  IDX_HEADS * IDX_DIM)

    hm_s, c_s, n_s, m_s = _mlstm_sample(qkv_s, small_s, poz_s, bif_lane, g_mlstm[0].astype(f32),
                                        state_C[0].astype(f32), state_n[0].astype(f32),
                                        state_m[0].astype(f32), tokens)

    cache_ik_t = jnp.swapaxes(cache_idx_k[0].astype(f32), 1, 2)
    scores_s = _ds_scores(page_table, iq_s, small_s, cache_ik_t, tokens)
    topk_s = min(TOPK_MAX, (past + tokens) // 4)
    ntile = ms // SUBLANE
    mask_s = _ds_mask(scores_s, topk_s, math.gcd(ntile, 8))
    tq = jnp.arange(tokens)
    by_kv = lambda b: jnp.transpose(b, (2, 0, 1)).reshape(A_KV_HEADS, A_GROUP * tokens, LANE)
    dist_last = (past + tq)[:, None] - (past - PAGE_SIZE + tl)[None, :]
    chunk_keys = KV_CHUNK_PAGES * PAGE_SIZE
    bias_last = jnp.concatenate([jnp.zeros((A_KV_HEADS, A_GROUP * tokens, chunk_keys - PAGE_SIZE), f32),
                                 by_kv(_bias_of(dist_last, rel_bias))], axis=2)
    bias_new = by_kv(_bias_of(tq[:, None] - (tl % tokens)[None, :], rel_bias))
    o_s = _ds_attend(page_table, q_s, k_rows_s, v_rows_s, mask_s, bias_last, bias_new, az_s,
                     cache_k[0].astype(f32), cache_v[0].astype(f32), tokens)
    y_s = _final(xs, hm_s, o_s, sg_s, wbm, wba, wo, tms)

    ck, cv, ci = cache_k.dtype, cache_v.dtype, cache_idx_k.dtype
    kv_shape_p = (1, bsz, seq, A_KV_HEADS, A_DIM)
    kv_shape_s = (1, nreq, tokens, A_KV_HEADS, A_DIM)
    return (y_p.reshape(bsz, seq, d).astype(x_prompt.dtype),
            y_s.reshape(nreq, tokens, d).astype(x_sample.dtype),
            k_rows.reshape(kv_shape_p).astype(ck),
            v_rows.reshape(kv_shape_p).astype(cv),
            small[:, SM_IK:SM_IK + IDX_DIM].reshape(1, bsz, seq, IDX_DIM).astype(ci),
            c_p.reshape(1, bsz, M_HEADS, M_DIM, M_DIM).astype(state_C.dtype),
            nm_p[:, :, 0, :].reshape(1, bsz, M_HEADS, M_DIM).astype(state_n.dtype),
            nm_p[:, :, 1, 0].reshape(1, bsz, M_HEADS).astype(state_m.dtype),
            k_rows_s.reshape(kv_shape_s).astype(ck),
            v_rows_s.reshape(kv_shape_s).astype(cv),
            small_s[:, SM_IK:SM_IK + IDX_DIM].reshape(1, nreq, tokens, IDX_DIM).astype(ci),
            c_s.reshape(1, nreq, M_HEADS, M_DIM, M_DIM).astype(state_C.dtype),
            n_s.reshape(1, nreq, M_HEADS, M_DIM).astype(state_n.dtype),
            m_s.reshape(1, nreq, M_HEADS).astype(state_m.dtype))
```

```python
import functools
import math

import jax
import jax.numpy as jnp
from jax import lax
from jax.experimental import pallas as pl
from jax.experimental.pallas import tpu as pltpu

M_HEADS = 4
M_DIM = 256
M_WIDTH = M_HEADS * M_DIM
A_HEADS = 8
A_KV_HEADS = 4
A_GROUP = A_HEADS // A_KV_HEADS
A_DIM = 128
A_WIDTH = A_HEADS * A_DIM
A_KV_WIDTH = A_KV_HEADS * A_DIM
IDX_HEADS = 4
IDX_DIM = 64
TOPK_MAX = 256
Q_BLOCK = 128
PAGE_SIZE = 128
REL_BUCKETS = 32
REL_MAX_DIST = 128
EPS = 1e-6

LANE = 128
SUBLANE = 8
VMEM_LIMIT = 56 * 1024 * 1024

SM_IK = 0
SM_I = IDX_DIM
SM_F = SM_I + M_HEADS
SM_W = SM_F + M_HEADS

f32 = jnp.float32
bf16 = jnp.bfloat16
i32 = jnp.int32

NEG_INF = float("-inf")
KEY_NEG_INF = -2139095041
KEY_UNSELECTED = KEY_NEG_INF
INT_MIN = -2147483648
INT_MAX = 2147483647


def _cparams(sem):
    return pltpu.CompilerParams(dimension_semantics=sem, vmem_limit_bytes=VMEM_LIMIT)


def _sigmoid(x):
    return 1.0 / (1.0 + jnp.exp(-x))


def _log_sigmoid(x):
    return jnp.minimum(x, 0.0) - jnp.log(1.0 + jnp.exp(-jnp.abs(x)))


def _float_key(x):
    b = lax.bitcast_convert_type(x + 0.0, i32)
    return jnp.where(b < 0, b ^ INT_MAX, b)


def _floor_avg(lo, hi):
    return (lo >> 1) + (hi >> 1) + (lo & hi & 1)


def _rmsnorm_kernel(x_ref, g_ref, o_ref):
    x = x_ref[...]
    ms = jnp.mean(x * x, axis=-1, keepdims=True)
    o_ref[...] = (x * lax.rsqrt(ms + EPS) * g_ref[...]).astype(bf16)


def _rmsnorm(x, g, tm):
    m, d = x.shape
    return pl.pallas_call(
        _rmsnorm_kernel,
        out_shape=jax.ShapeDtypeStruct((m, d), bf16),
        grid=(m // tm,),
        in_specs=[pl.BlockSpec((tm, d), lambda i: (i, 0)),
                  pl.BlockSpec((1, d), lambda i: (0, 0))],
        out_specs=pl.BlockSpec((tm, d), lambda i: (i, 0)),
        compiler_params=_cparams(("parallel",)),
        name="rmsnorm",
    )(x, g.reshape(1, d))


def _proj_cast_kernel(x_ref, w_ref, o_ref):
    o_ref[...] = jnp.dot(x_ref[...], w_ref[...], preferred_element_type=f32).astype(o_ref.dtype)


def _proj_dual_kernel(x_ref, w_ref, o_ref, ob_ref):
    a = jnp.dot(x_ref[...], w_ref[...], preferred_element_type=f32)
    o_ref[...] = a
    ob_ref[...] = a.astype(bf16)


def _proj_act_kernel(x_ref, w_ref, o_ref, *, act):
    a = jnp.dot(x_ref[...], w_ref[...], preferred_element_type=f32)
    if act == "silu":
        a = a * _sigmoid(a)
    elif act == "sigmoid":
        a = _sigmoid(a)
    o_ref[...] = a


def _proj_oz_kernel(x_ref, wo_ref, wz_ref, o_ref):
    x = x_ref[...]
    o = jnp.dot(x, wo_ref[...], preferred_element_type=f32)
    z = jnp.dot(x, wz_ref[...], preferred_element_type=f32)
    o_ref[...] = _sigmoid(o) * (z * _sigmoid(z))


def _proj_knorm_kernel(x_ref, w_ref, g_ref, o_ref, ob_ref):
    a = jnp.dot(x_ref[...], w_ref[...], preferred_element_type=f32)
    g = g_ref[...]
    for c in range(a.shape[1] // A_DIM):
        blk = a[:, c * A_DIM:(c + 1) * A_DIM]
        ms = jnp.mean(blk * blk, axis=-1, keepdims=True)
        kn = blk * lax.rsqrt(ms + EPS) * g
        o_ref[:, c * A_DIM:(c + 1) * A_DIM] = kn
        ob_ref[:, c * A_DIM:(c + 1) * A_DIM] = kn.astype(bf16)


def _proj_qnorm_kernel(x_ref, w_ref, g_ref, o_ref):
    a = jnp.dot(x_ref[...], w_ref[...], preferred_element_type=f32)
    g = g_ref[...]
    for c in range(a.shape[1] // A_DIM):
        blk = a[:, c * A_DIM:(c + 1) * A_DIM]
        ms = jnp.mean(blk * blk, axis=-1, keepdims=True)
        o_ref[:, c * A_DIM:(c + 1) * A_DIM] = (blk * lax.rsqrt(ms + EPS) * g).astype(o_ref.dtype)


def _proj_nat(body, xn, weights, extras, outs, tm, tn, name):
    m, d = xn.shape
    n = weights[0].shape[1]
    in_specs = [pl.BlockSpec((tm, d), lambda j, i: (i, 0))]
    in_specs += [pl.BlockSpec((d, tn), lambda j, i: (0, j)) for _ in weights]
    in_specs += [pl.BlockSpec((1, e.shape[1]), lambda j, i: (0, 0)) for e in extras]
    out_shape = [jax.ShapeDtypeStruct((m, n), dt) for dt in outs]
    out_specs = [pl.BlockSpec((tm, tn), lambda j, i: (i, j)) for _ in outs]
    res = pl.pallas_call(
        body,
        out_shape=out_shape,
        grid=(n // tn, m // tm),
        in_specs=in_specs,
        out_specs=out_specs,
        compiler_params=_cparams(("parallel", "parallel")),
        name=name,
    )(xn, *weights, *extras)
    return res


def _projT_kernel(x_ref, wt_ref, *rest, kind):
    if kind == "qnorm":
        g_ref, o_ref = rest
    else:
        (o_ref,) = rest
    a = lax.dot_general(wt_ref[...], x_ref[...], (((1,), (1,)), ((), ())),
                        preferred_element_type=f32)
    tn, tm = a.shape
    if kind == "qnorm":
        parts = []
        for c in range(tn // A_DIM):
            blk = a[c * A_DIM:(c + 1) * A_DIM, :]
            ms = jnp.mean(blk * blk, axis=0, keepdims=True)
            parts.append(blk * lax.rsqrt(ms + EPS))
        a = jnp.concatenate(parts, axis=0) if len(parts) > 1 else parts[0]
        g = g_ref[...]
    for c in range(tm // LANE):
        blk = a[:, c * LANE:(c + 1) * LANE]
        if kind == "qnorm":
            blk = blk * g
        o_ref[c] = blk.astype(o_ref.dtype)


def _proj_T(xn, wt, g, kind, out_dtype, tm, tn, name):
    m, d = xn.shape
    n = wt.shape[0]
    in_specs = [pl.BlockSpec((tm, d), lambda j, i: (i, 0)),
                pl.BlockSpec((tn, d), lambda j, i: (j, 0))]
    args = [xn, wt]
    if kind == "qnorm":
        in_specs.append(pl.BlockSpec((tn, LANE), lambda j, i: (j, 0)))
        args.append(g)
    return pl.pallas_call(
        functools.partial(_projT_kernel, kind=kind),
        out_shape=jax.ShapeDtypeStruct((m // LANE, n, LANE), out_dtype),
        grid=(n // tn, m // tm),
        in_specs=in_specs,
        out_specs=pl.BlockSpec((tm // LANE, tn, LANE), lambda j, i: (i, j, 0)),
        compiler_params=_cparams(("parallel", "parallel")),
        name=name,
    )(*args)


def _col_to_row(col, eye):
    return jnp.sum(jnp.where(eye, col, 0.0), axis=0, keepdims=True)


def _mlstm_head(q, k, v, ig_col, lf_col, c_prev, n_prev, m_prev, tri, eye):
    L = q.shape[0]
    ig_row = _col_to_row(ig_col, eye)
    lf_row = _col_to_row(lf_col, eye)
    b_col = jnp.sum(jnp.where(tri, lf_row, 0.0), axis=1, keepdims=True)
    b_row = _col_to_row(b_col, eye)
    dmat = jnp.where(tri, b_col - b_row + ig_row, NEG_INF)
    inter = b_col + m_prev
    m_t = jnp.maximum(inter, jnp.max(dmat, axis=1, keepdims=True))
    w_intra = jnp.exp(dmat - m_t)
    w_inter = jnp.exp(inter - m_t)
    s = lax.dot_general(q, k, (((1,), (1,)), ((), ())), preferred_element_type=f32) * w_intra
    num = w_inter * jnp.dot(q, c_prev.astype(bf16), preferred_element_type=f32)
    num = num + jnp.dot(s.astype(bf16), v, preferred_element_type=f32)
    den = w_inter * jnp.sum(q.astype(f32) * n_prev, axis=1, keepdims=True)
    den = den + jnp.sum(s, axis=1, keepdims=True)
    h = num / jnp.maximum(jnp.abs(den), jnp.exp(-m_t))
    b_last = b_row[:, L - 1:L]
    dec_row = b_last - b_row + ig_row
    m_new = jnp.maximum(b_last + m_prev, jnp.max(dec_row, axis=1, keepdims=True))
    wk_row = jnp.exp(dec_row - m_new)
    sc = jnp.exp(b_last + m_prev - m_new)
    wk_col = jnp.exp(b_last - b_col + ig_col - m_new)
    kw = (k.astype(f32) * wk_col).astype(bf16)
    c_new = sc * c_prev + lax.dot_general(kw, v, (((0,), (0,)), ((), ())),
                                          preferred_element_type=f32)
    wk8 = jnp.broadcast_to(wk_row, (SUBLANE, L)).astype(bf16)
    n_new = sc * n_prev + jnp.dot(wk8, k, preferred_element_type=f32)[0:1, :]
    return h, c_new, n_new, m_new


def _mlstm_prompt_kernel(qkv_q, qkv_k, qkv_v, sm_ref, poz_ref, bif_ref, g_ref,
                         hm_ref, c_out, nm_out, c_s, n_s, m_s):
    ci = pl.program_id(1)
    L = qkv_q.shape[0]

    @pl.when(ci == 0)
    def _():
        c_s[...] = jnp.zeros_like(c_s)
        n_s[...] = jnp.zeros_like(n_s)
        m_s[...] = jnp.zeros_like(m_s)

    r = lax.broadcasted_iota(i32, (L, L), 0)
    c = lax.broadcasted_iota(i32, (L, L), 1)
    tri = c <= r
    eye = c == r
    sm = sm_ref[...] + bif_ref[...]
    for h in range(M_HEADS):
        ig_col = sm[:, SM_I + h:SM_I + h + 1]
        lf_col = _log_sigmoid(sm[:, SM_F + h:SM_F + h + 1])
        sl = slice(h * M_DIM, (h + 1) * M_DIM)
        hh, c_new, n_new, m_new = _mlstm_head(
            qkv_q[:, sl], qkv_k[:, sl], qkv_v[:, sl], ig_col, lf_col,
            c_s[h], n_s[h][0:1, :], m_s[h][0:1, 0:1], tri, eye)
        c_s[h] = c_new
        n_s[h] = jnp.broadcast_to(n_new, (SUBLANE, M_DIM))
        m_s[h] = jnp.broadcast_to(m_new, (SUBLANE, LANE))
        ms = jnp.mean(hh * hh, axis=-1, keepdims=True)
        hn = hh * lax.rsqrt(ms + EPS) * g_ref[:, sl]
        hm_ref[:, sl] = (hn * poz_ref[:, sl]).astype(bf16)

    @pl.when(ci == pl.num_programs(1) - 1)
    def _():
        c_out[0] = c_s[...]
        for h in range(M_HEADS):
            nm_out[0, h] = jnp.concatenate(
                [n_s[h][0:1, :], jnp.broadcast_to(m_s[h][0:1, 0:1], (SUBLANE - 1, M_DIM))], axis=0)


def _mlstm_prompt(qkv, small, poz, bif_lane, g_mlstm, bsz, seq, L):
    nc = seq // L
    m = bsz * seq
    blk = lambda off: pl.BlockSpec((L, M_WIDTH), lambda b, c, off=off: (b * nc + c, off))
    return pl.pallas_call(
        _mlstm_prompt_kernel,
        out_shape=[jax.ShapeDtypeStruct((m, M_WIDTH), bf16),
                   jax.ShapeDtypeStruct((bsz, M_HEADS, M_DIM, M_DIM), f32),
                   jax.ShapeDtypeStruct((bsz, M_HEADS, SUBLANE, M_DIM), f32)],
        grid=(bsz, nc),
        in_specs=[blk(0), blk(1), blk(2),
                  pl.BlockSpec((L, LANE), lambda b, c: (b * nc + c, 0)),
                  pl.BlockSpec((L, M_WIDTH), lambda b, c: (b * nc + c, 0)),
                  pl.BlockSpec((1, LANE), lambda b, c: (0, 0)),
                  pl.BlockSpec((1, M_WIDTH), lambda b, c: (0, 0))],
        out_specs=[pl.BlockSpec((L, M_WIDTH), lambda b, c: (b * nc + c, 0)),
                   pl.BlockSpec((1, M_HEADS, M_DIM, M_DIM), lambda b, c: (b, 0, 0, 0)),
                   pl.BlockSpec((1, M_HEADS, SUBLANE, M_DIM), lambda b, c: (b, 0, 0, 0))],
        scratch_shapes=[pltpu.VMEM((M_HEADS, M_DIM, M_DIM), f32),
                        pltpu.VMEM((M_HEADS, SUBLANE, M_DIM), f32),
                        pltpu.VMEM((M_HEADS, SUBLANE, LANE), f32)],
        compiler_params=_cparams(("parallel", "arbitrary")),
        name="mlstm_prompt",
    )(qkv, qkv, qkv, small, poz, bif_lane, g_mlstm.reshape(1, M_WIDTH))


SAMPLE_PAD = 128


def _mlstm_sample_kernel(q_ref, k_ref, v_ref, sm_ref, poz_ref, bif_ref, g_ref, c_in, n_in, m_in,
                         hm_ref, c_out, n_out, m_out, *, tokens):
    L = SAMPLE_PAD
    rows = q_ref.shape[0]
    reqs = rows // tokens
    r = lax.broadcasted_iota(i32, (L, L), 0)
    c = lax.broadcasted_iota(i32, (L, L), 1)
    tri = c <= r
    eye = c == r
    row = lax.broadcasted_iota(i32, (L, 1), 0)
    row8 = lax.broadcasted_iota(i32, (rows, 1), 0)

    def pad(a):
        return jnp.concatenate([a, jnp.zeros((L - rows, a.shape[1]), a.dtype)], axis=0)

    sm = pad(sm_ref[...] + bif_ref[...])
    qp = pad(q_ref[...])
    kp = pad(k_ref[...])
    vp = pad(v_ref[...])
    for h in range(M_HEADS):
        sl = slice(h * M_DIM, (h + 1) * M_DIM)
        q = qp[:, sl].astype(bf16)
        k = kp[:, sl].astype(bf16)
        v = vp[:, sl].astype(bf16)
        h_tile = jnp.zeros((rows, M_DIM), f32)
        for rr in range(reqs):
            mine = (row >= rr * tokens) & (row < (rr + 1) * tokens)
            ig_col = jnp.where(mine, sm[:, SM_I + h:SM_I + h + 1], NEG_INF)
            lf_col = jnp.where(mine, _log_sigmoid(sm[:, SM_F + h:SM_F + h + 1]), 0.0)
            hh, c_new, n_new, m_new = _mlstm_head(
                q, k, v, ig_col, lf_col, c_in[rr, h], n_in[rr, h], m_in[rr, h], tri, eye)
            c_out[rr, h] = c_new
            n_out[rr, h] = n_new
            m_out[rr, h] = m_new
            mine8 = (row8 >= rr * tokens) & (row8 < (rr + 1) * tokens)
            h_tile = jnp.where(mine8, hh[0:rows, :], h_tile)
        ms = jnp.mean(h_tile * h_tile, axis=-1, keepdims=True)
        hn = h_tile * lax.rsqrt(ms + EPS) * g_ref[:, sl]
        hm_ref[:, sl] = hn * poz_ref[:, sl]


def _mlstm_sample(qkv, small, poz, bif_lane, g_mlstm, state_c, state_n, state_m, tokens):
    m = qkv.shape[0]
    rows = SUBLANE
    reqs = rows // tokens
    nreq = m // tokens
    blk = lambda off: pl.BlockSpec((rows, M_WIDTH), lambda i, off=off: (i, off))
    n4 = state_n.reshape(nreq, M_HEADS, 1, M_DIM)
    m4 = state_m.reshape(nreq, M_HEADS, 1, 1)
    return pl.pallas_call(
        functools.partial(_mlstm_sample_kernel, tokens=tokens),
        out_shape=[jax.ShapeDtypeStruct((m, M_WIDTH), f32),
                   jax.ShapeDtypeStruct(state_c.shape, f32),
                   jax.ShapeDtypeStruct(n4.shape, f32),
                   jax.ShapeDtypeStruct(m4.shape, f32)],
        grid=(m // rows,),
        in_specs=[blk(0), blk(1), blk(2),
                  pl.BlockSpec((rows, LANE), lambda i: (i, 0)),
                  pl.BlockSpec((rows, M_WIDTH), lambda i: (i, 0)),
                  pl.BlockSpec((1, LANE), lambda i: (0, 0)),
                  pl.BlockSpec((1, M_WIDTH), lambda i: (0, 0)),
                  pl.BlockSpec((reqs, M_HEADS, M_DIM, M_DIM), lambda i: (i, 0, 0, 0)),
                  pl.BlockSpec((reqs, M_HEADS, 1, M_DIM), lambda i: (i, 0, 0, 0)),
                  pl.BlockSpec((reqs, M_HEADS, 1, 1), lambda i: (i, 0, 0, 0))],
        out_specs=[pl.BlockSpec((rows, M_WIDTH), lambda i: (i, 0)),
                   pl.BlockSpec((reqs, M_HEADS, M_DIM, M_DIM), lambda i: (i, 0, 0, 0)),
                   pl.BlockSpec((reqs, M_HEADS, 1, M_DIM), lambda i: (i, 0, 0, 0)),
                   pl.BlockSpec((reqs, M_HEADS, 1, 1), lambda i: (i, 0, 0, 0))],
        compiler_params=_cparams(("parallel",)),
        name="mlstm_sample",
    )(qkv, qkv, qkv, small, poz, bif_lane, g_mlstm.reshape(1, M_WIDTH), state_c, n4, m4)


def _kth_key(count_ge, kk, shape):
    def cond(carry):
        _, _, done, it = carry
        return (it < 32) & (jnp.max(jnp.where(done > 0, 0.0, 1.0)) > 0.0)

    def body(carry):
        lo, hi, done, it = carry
        mid = _floor_avg(lo, hi)
        cnt = count_ge(mid)
        ge = cnt >= kk
        live = done == 0
        stop = (cnt == kk) | (mid == lo)
        lo = jnp.where(live & ge, mid, lo)
        hi = jnp.where(live & jnp.logical_not(ge), mid, hi)
        return lo, hi, jnp.where(stop, 1, done), it + 1

    lo, _, _, _ = lax.while_loop(cond, body, (jnp.full(shape, INT_MIN, i32), jnp.full(shape, INT_MAX, i32),
                                              jnp.zeros(shape, i32), jnp.int32(0)))
    return lo


def _dsa_prompt_kernel(k_ref, vt_ref, ik_ref, qt_ref, iqt_ref, smt_ref, bias_ref, az_ref,
                       o_ref, key_s, acc_s, m_s, l_s, *, topk, nsub):
    qi = pl.program_id(1)
    nblk = qi + 1
    TQ = nsub * LANE
    TK = nsub * LANE
    t_abs = qi * TQ + lax.broadcasted_iota(i32, (1, TQ), 1)
    s_loc = lax.broadcasted_iota(i32, (LANE, 1), 0)

    def lanes(ref3, base):
        return jnp.concatenate([ref3[base + u] for u in range(nsub)], axis=1)

    iqt = lanes(iqt_ref, 0)
    w_rows = lanes(smt_ref, 0)[SM_W:SM_W + SUBLANE, :]

    def col_sum(a):
        out = a[0:SUBLANE, :]
        for u in range(1, a.shape[0] // SUBLANE):
            out = out + a[u * SUBLANE:(u + 1) * SUBLANE, :]
        return out

    def score_blk(j, _):
        for u in range(nsub):
            rows = pl.ds(j * TK + u * LANE, LANE)
            ikb = ik_ref[rows, :][:, SM_IK:SM_IK + IDX_DIM]
            acc = jnp.zeros((LANE, TQ), f32)
            for h in range(IDX_HEADS):
                s = jnp.dot(ikb, iqt[h * IDX_DIM:(h + 1) * IDX_DIM, :], preferred_element_type=f32)
                acc = acc + jnp.maximum(s, 0.0) * w_rows[h:h + 1, :]
            acc = jnp.where(j * TK + u * LANE + s_loc <= t_abs, acc, NEG_INF)
            key_s[rows, :] = _float_key(acc)
        return 0

    lax.fori_loop(0, nblk, score_blk, 0)

    def count_ge(th):
        def body(j, acc):
            return acc + col_sum(jnp.where(key_s[pl.ds(j * TK, TK), :] >= th, 1.0, 0.0))
        acc = lax.fori_loop(0, nblk, body, jnp.zeros((SUBLANE, TQ), f32))
        return jnp.sum(acc, axis=0, keepdims=True)

    nvalid = (t_abs + 1).astype(f32)
    kk = jnp.minimum(float(topk), nvalid)
    th = _kth_key(count_ge, kk, (1, TQ))
    th = jnp.where(nvalid <= float(topk), KEY_NEG_INF + 1, th)

    tie = (nvalid > float(topk)) & (count_ge(th) > kk)
    any_tie = jnp.max(jnp.where(tie, 1.0, 0.0))

    @pl.when(any_tie > 0.0)
    def _():
        need = jnp.where(tie, kk - count_ge(th + 1), float(INT_MAX))
        rr = lax.broadcasted_iota(i32, (LANE, LANE), 0)
        cc = lax.broadcasted_iota(i32, (LANE, LANE), 1)
        lower = jnp.where(cc < rr, 1.0, 0.0).astype(bf16)

        def body(j, run):
            rows = pl.ds(j * LANE, LANE)
            kb = key_s[rows, :]
            eq = kb == th
            eqf = jnp.where(eq, 1.0, 0.0)
            rank = run + jnp.dot(lower, eqf.astype(bf16), preferred_element_type=f32)
            key_s[rows, :] = jnp.where(eq & (rank >= need), KEY_UNSELECTED, kb)
            return run + jnp.sum(eqf, axis=0, keepdims=True)

        lax.fori_loop(0, nblk * nsub, body, jnp.zeros((1, TQ), f32))

    m_s[...] = jnp.full(m_s.shape, NEG_INF, f32)
    l_s[...] = jnp.zeros_like(l_s)
    acc_s[...] = jnp.zeros_like(acc_s)
    qt = lanes(qt_ref, 0)

    def att_blk(j, near):
        sel = key_s[pl.ds(j * TK, TK), :] >= th
        kb = k_ref[pl.ds(j * TK, TK), :]
        vtb = lanes(vt_ref, j * nsub)
        for h in range(A_HEADS):
            kv = h // A_GROUP
            lg = jnp.dot(kb[:, kv * A_DIM:(kv + 1) * A_DIM], qt[h * A_DIM:(h + 1) * A_DIM, :],
                         preferred_element_type=f32)
            if near is not None:
                lg = lg + bias_ref[near, h]
            lg = jnp.where(sel, lg, NEG_INF)
            m_old = m_s[h][0:1, :]
            m_new = jnp.maximum(m_old, jnp.max(lg, axis=0, keepdims=True))
            m_safe = jnp.where(m_new == NEG_INF, 0.0, m_new)
            p = jnp.exp(lg - m_safe)
            alpha = jnp.exp(m_old - m_safe)
            l_new = alpha * l_s[h][0:1, :] + jnp.sum(p, axis=0, keepdims=True)
            acc_s[h] = alpha * acc_s[h] + jnp.dot(vtb[kv * A_DIM:(kv + 1) * A_DIM, :], p.astype(bf16),
                                                  preferred_element_type=f32)
            m_s[h] = jnp.broadcast_to(m_new, (SUBLANE, TQ))
            l_s[h] = jnp.broadcast_to(l_new, (SUBLANE, TQ))

    def far_blk(j, _):
        att_blk(j, None)
        return 0

    lax.fori_loop(0, qi - 1, far_blk, 0)

    @pl.when(qi >= 1)
    def _():
        att_blk(qi - 1, 1)

    att_blk(qi, 0)

    for h in range(A_HEADS):
        sl = slice(h * A_DIM, (h + 1) * A_DIM)
        ot = acc_s[h] / l_s[h][0:1, :]
        o_ref[:, sl] = (ot.T * az_ref[:, sl]).astype(o_ref.dtype)


def _dsa_prompt(k_bf, vt, ik_bf, qt, iqt, smallt, bias_tiles, az, bsz, seq, topk, nsub):
    tb = nsub * LANE
    nq = seq // tb
    m = bsz * seq
    sub3 = lambda w: pl.BlockSpec((nsub, w, LANE), lambda b, i: (b * nq + i, 0, 0))
    return pl.pallas_call(
        functools.partial(_dsa_prompt_kernel, topk=topk, nsub=nsub),
        out_shape=jax.ShapeDtypeStruct((m, A_WIDTH), bf16),
        grid=(bsz, nq),
        in_specs=[pl.BlockSpec((seq, A_KV_WIDTH), lambda b, i: (b, 0)),
                  pl.BlockSpec((seq // LANE, A_KV_WIDTH, LANE), lambda b, i: (b, 0, 0)),
                  pl.BlockSpec((seq, LANE), lambda b, i: (b, 0)),
                  sub3(A_WIDTH), sub3(IDX_HEADS * IDX_DIM), sub3(LANE),
                  pl.BlockSpec(bias_tiles.shape, lambda b, i: (0, 0, 0, 0)),
                  pl.BlockSpec((tb, A_WIDTH), lambda b, i: (b * nq + i, 0))],
        out_specs=pl.BlockSpec((tb, A_WIDTH), lambda b, i: (b * nq + i, 0)),
        scratch_shapes=[pltpu.VMEM((seq, tb), i32),
                        pltpu.VMEM((A_HEADS, A_DIM, tb), f32),
                        pltpu.VMEM((A_HEADS, SUBLANE, tb), f32),
                        pltpu.VMEM((A_HEADS, SUBLANE, tb), f32)],
        compiler_params=_cparams(("parallel", "arbitrary")),
        name="dsa_prompt",
    )(k_bf, vt, ik_bf, qt, iqt, smallt, bias_tiles, az)


KV_CHUNK_PAGES = 8
SCORE_GROUP_PAGES = 8


def _ds_scores_kernel(pt_ref, iq_ref, sm_ref, cache_ref, o_ref, ikbuf, sem, *, tokens, npages):
    pair = pl.program_id(0)
    rows = iq_ref.shape[0]
    reqs = rows // tokens
    sm = sm_ref[...]
    iq = iq_ref[...]
    iq_heads = jnp.concatenate([iq[:, h * IDX_DIM:(h + 1) * IDX_DIM] for h in range(IDX_HEADS)],
                               axis=0).astype(bf16)
    row = lax.broadcasted_iota(i32, (rows, 1), 0)

    def page_copy(step, par, rr, p):
        page = pt_ref[(step * reqs + rr) * npages + p]
        return pltpu.make_async_copy(cache_ref.at[page], ikbuf.at[par, rr, p], sem.at[par, rr])

    def issue_all(step, par):
        for rr in range(reqs):
            def issue(p, _, rr=rr):
                page_copy(step, par, rr, p).start()
                return 0
            lax.fori_loop(0, npages, issue, 0)

    def head_sum(s):
        acc = jnp.zeros((rows, s.shape[1]), f32)
        for h in range(IDX_HEADS):
            acc = acc + jnp.maximum(s[h * rows:(h + 1) * rows, :], 0.0) * sm[:, SM_W + h:SM_W + h + 1]
        return acc

    par = pair % 2

    @pl.when(pair == 0)
    def _():
        issue_all(pair, par)

    @pl.when(pair + 1 < pl.num_programs(0))
    def _():
        issue_all(pair + 1, 1 - par)

    for rr in range(reqs):
        def drain(p, _, rr=rr):
            page_copy(pair, par, rr, p).wait()
            return 0
        lax.fori_loop(0, npages, drain, 0)
        mine = (row >= rr * tokens) & (row < (rr + 1) * tokens)

        def group_scores(gi, _, rr=rr, mine=mine):
            p0 = gi * SCORE_GROUP_PAGES
            keys = jnp.concatenate([ikbuf[par, rr, p0 + j] for j in range(SCORE_GROUP_PAGES)], axis=1)
            sc = head_sum(jnp.dot(iq_heads, keys.astype(bf16), preferred_element_type=f32))
            for j in range(SCORE_GROUP_PAGES):
                blk = sc[:, j * PAGE_SIZE:(j + 1) * PAGE_SIZE]
                if rr == 0:
                    o_ref[0, p0 + j] = blk
                else:
                    o_ref[0, p0 + j] = jnp.where(mine, blk, o_ref[0, p0 + j])
            return 0
        lax.fori_loop(0, npages // SCORE_GROUP_PAGES, group_scores, 0)

    ik_new = jnp.concatenate([sm[:, SM_IK:SM_IK + IDX_DIM],
                              jnp.zeros((LANE - rows, IDX_DIM), f32)], axis=0).astype(bf16)
    s = lax.dot_general(iq_heads, ik_new, (((1,), (1,)), ((), ())), preferred_element_type=f32)
    col = lax.broadcasted_iota(i32, (1, LANE), 1)
    ok = (col < rows) & ((col // tokens) == (row // tokens)) & (col <= row)
    o_ref[0, npages] = jnp.where(ok, head_sum(s), NEG_INF)


def _ds_scores(page_table, iq, small, cache_ik_t, tokens):
    m = iq.shape[0]
    nreq, npages = page_table.shape
    rows = SUBLANE
    reqs = rows // tokens
    return pl.pallas_call(
        functools.partial(_ds_scores_kernel, tokens=tokens, npages=npages),
        out_shape=jax.ShapeDtypeStruct((m // rows, npages + 1, rows, LANE), f32),
        grid_spec=pltpu.PrefetchScalarGridSpec(
            num_scalar_prefetch=1,
            grid=(m // rows,),
            in_specs=[pl.BlockSpec((rows, IDX_HEADS * IDX_DIM), lambda i, pt: (i, 0)),
                      pl.BlockSpec((rows, LANE), lambda i, pt: (i, 0)),
                      pl.BlockSpec(memory_space=pl.ANY)],
            out_specs=pl.BlockSpec((1, npages + 1, rows, LANE), lambda i, pt: (i, 0, 0, 0)),
            scratch_shapes=[pltpu.VMEM((2, reqs, npages, IDX_DIM, PAGE_SIZE), f32),
                            pltpu.SemaphoreType.DMA((2, reqs))]),
        compiler_params=_cparams(("arbitrary",)),
        name="dsa_sample_scores",
    )(page_table.reshape(-1), iq, small, cache_ik_t)


def _ds_mask_kernel(sc_ref, o_ref, key_s, *, topk):
    tb, p1 = sc_ref.shape[0], sc_ref.shape[1]
    key_s[...] = _float_key(sc_ref[...])

    def count_ge(th):
        ind = jnp.where(key_s[...] >= th, 1.0, 0.0)
        return jnp.sum(jnp.sum(ind, axis=1, keepdims=True), axis=3, keepdims=True)

    shape = (tb, 1, SUBLANE, 1)
    nvalid = count_ge(jnp.full(shape, KEY_NEG_INF + 1, i32))
    kk = jnp.minimum(float(topk), nvalid)
    th = _kth_key(count_ge, kk, shape)
    th = jnp.where(nvalid <= float(topk), KEY_NEG_INF + 1, th)
    tie = (nvalid > float(topk)) & (count_ge(th) > kk)
    any_tie = jnp.max(jnp.where(tie, 1.0, 0.0))

    @pl.when(any_tie > 0.0)
    def _():
        need = jnp.where(tie, kk - count_ge(th + 1), float(INT_MAX))
        rr = lax.broadcasted_iota(i32, (LANE, LANE), 0)
        cc = lax.broadcasted_iota(i32, (LANE, LANE), 1)
        upper = jnp.where(rr < cc, 1.0, 0.0).astype(bf16)

        def page(p, run):
            new = []
            for t in range(tb):
                kb = key_s[t, p]
                eq = kb == th[t, 0]
                eqf = jnp.where(eq, 1.0, 0.0)
                rank = run[t] + jnp.dot(eqf.astype(bf16), upper, preferred_element_type=f32)
                key_s[t, p] = jnp.where(eq & (rank >= need[t, 0]), KEY_UNSELECTED, kb)
                new.append(run[t] + jnp.sum(eqf, axis=1, keepdims=True))
            return tuple(new)

        lax.fori_loop(0, p1, page, tuple(jnp.zeros((SUBLANE, 1), f32) for _ in range(tb)))

    o_ref[...] = jnp.where(key_s[...] >= th, 0.0, NEG_INF)


def _ds_mask(scores, topk, tb):
    nt, p1, rows, lanes = scores.shape
    blk = pl.BlockSpec((tb, p1, rows, lanes), lambda i: (i, 0, 0, 0))
    return pl.pallas_call(
        functools.partial(_ds_mask_kernel, topk=topk),
        out_shape=jax.ShapeDtypeStruct(scores.shape, f32),
        grid=(nt // tb,),
        in_specs=[blk],
        out_specs=blk,
        scratch_shapes=[pltpu.VMEM((tb, p1, rows, lanes), i32)],
        compiler_params=_cparams(("parallel",)),
        name="dsa_sample_mask",
    )(scores)


def _ds_attend_kernel(pt_ref, q_ref, knew_ref, vnew_ref, mask_ref, biasl_ref, biasn_ref, az_ref,
                      kc_ref, vc_ref, o_ref, kbuf, vbuf, sem, *, tokens, npages):
    pair = pl.program_id(0)
    rows = q_ref.shape[0]
    reqs = rows // tokens
    nchunk = npages // KV_CHUNK_PAGES
    total = reqs * nchunk
    row = lax.broadcasted_iota(i32, (rows, 1), 0)
    lo_half = row < tokens

    def half_swap(a):
        return pltpu.roll(a, tokens, axis=0)

    def own_rows(a, rr):
        return jnp.where(lo_half, a, half_swap(a)) if rr == 0 else jnp.where(lo_half, half_swap(a), a)

    page_rows = PAGE_SIZE * A_KV_HEADS
    chunk = KV_CHUNK_PAGES * PAGE_SIZE

    def chunk_copies(g, slot):
        cps = []
        for p in range(KV_CHUNK_PAGES):
            page = pt_ref[g * KV_CHUNK_PAGES + p]
            dst = pl.ds(p * page_rows, page_rows)
            cps.append(pltpu.make_async_copy(kc_ref.at[page], kbuf.at[slot, dst], sem.at[slot, 0]))
            cps.append(pltpu.make_async_copy(vc_ref.at[page], vbuf.at[slot, dst], sem.at[slot, 1]))
        return cps

    assert total % 2 == 0
    first = pair * total

    @pl.when(pair == 0)
    def _():
        for cp in chunk_copies(0, 0):
            cp.start()

    def update(state, lg, vb):
        m_old, l_old, acc = state
        m_new = jnp.maximum(m_old, jnp.max(lg, axis=1, keepdims=True))
        m_safe = jnp.where(m_new == NEG_INF, 0.0, m_new)
        p = jnp.exp(lg - m_safe)
        alpha = jnp.exp(m_old - m_safe)
        l_new = alpha * l_old + jnp.sum(p, axis=1, keepdims=True)
        acc = alpha * acc + jnp.dot(p.astype(bf16), vb, preferred_element_type=f32)
        return m_new, l_new, acc

    def pad_rows(a):
        return jnp.concatenate([a, jnp.zeros((LANE - a.shape[0], a.shape[1]), a.dtype)], axis=0).astype(bf16)

    q = q_ref[...]
    out = [jnp.zeros((rows, A_DIM), f32) for _ in range(A_HEADS)]
    for rr in range(reqs):
        qk = []
        for kv in range(A_KV_HEADS):
            a = q[:, (A_GROUP * kv) * A_DIM:(A_GROUP * kv + 1) * A_DIM]
            b = q[:, (A_GROUP * kv + 1) * A_DIM:(A_GROUP * kv + 2) * A_DIM]
            qkv = jnp.where(lo_half, a, half_swap(b)) if rr == 0 else jnp.where(lo_half, half_swap(a), b)
            qk.append(qkv.astype(bf16))

        def body(c, state, rr=rr, qk=qk):
            g = first + rr * nchunk + c
            slot = g % 2

            @pl.when(g + 1 < pl.num_programs(0) * total)
            def _():
                for cp in chunk_copies(g + 1, 1 - slot):
                    cp.start()

            for cp in chunk_copies(g, slot):
                cp.wait()
            mtile = jnp.concatenate([mask_ref[0, c * KV_CHUNK_PAGES + p] for p in range(KV_CHUNK_PAGES)],
                                    axis=1)
            m8 = own_rows(mtile, rr)
            last = jnp.where(c == nchunk - 1, 1.0, 0.0)
            new_state = []
            for kv in range(A_KV_HEADS):
                head_rows = pl.ds(kv, chunk, stride=A_KV_HEADS)
                kb = kbuf[slot, head_rows, :].astype(bf16)
                vb = vbuf[slot, head_rows, :].astype(bf16)
                lg = lax.dot_general(qk[kv], kb, (((1,), (1,)), ((), ())), preferred_element_type=f32)
                lg = lg + last * biasl_ref[kv] + m8
                new_state.append(update(state[kv], lg, vb))
            return tuple(new_state)

        state0 = tuple((jnp.full((rows, 1), NEG_INF, f32), jnp.zeros((rows, 1), f32),
                        jnp.zeros((rows, A_DIM), f32)) for _ in range(A_KV_HEADS))
        state = lax.fori_loop(0, nchunk, body, state0)

        m8n = own_rows(mask_ref[0, npages], rr)
        for kv in range(A_KV_HEADS):
            sl = slice(kv * A_DIM, (kv + 1) * A_DIM)
            lg = lax.dot_general(qk[kv], pad_rows(knew_ref[:, sl]), (((1,), (1,)), ((), ())),
                                 preferred_element_type=f32)
            lg = lg + biasn_ref[kv] + m8n
            _, l_fin, acc = update(state[kv], lg, pad_rows(vnew_ref[:, sl]))
            o_kv = acc / l_fin
            h0, h1 = A_GROUP * kv, A_GROUP * kv + 1
            if rr == 0:
                out[h0] = jnp.where(lo_half, o_kv, out[h0])
                out[h1] = jnp.where(lo_half, half_swap(o_kv), out[h1])
            else:
                out[h0] = jnp.where(lo_half, out[h0], half_swap(o_kv))
                out[h1] = jnp.where(lo_half, out[h1], o_kv)

    for h in range(A_HEADS):
        sl = slice(h * A_DIM, (h + 1) * A_DIM)
        o_ref[:, sl] = out[h] * az_ref[:, sl]


def _ds_attend(page_table, q, k_new, v_new, mask, bias_last, bias_new, az, cache_k, cache_v, tokens):
    nreq, npages = page_table.shape
    m = q.shape[0]
    rows = SUBLANE
    chunk = KV_CHUNK_PAGES * PAGE_SIZE
    tile = lambda w: pl.BlockSpec((rows, w), lambda i, pt: (i, 0))
    const3 = lambda a: pl.BlockSpec(a.shape, lambda i, pt: (0, 0, 0))
    return pl.pallas_call(
        functools.partial(_ds_attend_kernel, tokens=tokens, npages=npages),
        out_shape=jax.ShapeDtypeStruct((m, A_WIDTH), f32),
        grid_spec=pltpu.PrefetchScalarGridSpec(
            num_scalar_prefetch=1,
            grid=(m // rows,),
            in_specs=[tile(A_WIDTH), tile(A_KV_WIDTH), tile(A_KV_WIDTH),
                      pl.BlockSpec((1, npages + 1, rows, LANE), lambda i, pt: (i, 0, 0, 0)),
                      const3(bias_last), const3(bias_new), tile(A_WIDTH),
                      pl.BlockSpec(memory_space=pl.ANY),
                      pl.BlockSpec(memory_space=pl.ANY)],
            out_specs=tile(A_WIDTH),
            scratch_shapes=[pltpu.VMEM((2, chunk * A_KV_HEADS, A_DIM), f32),
                            pltpu.VMEM((2, chunk * A_KV_HEADS, A_DIM), f32),
                            pltpu.SemaphoreType.DMA((2, 2))]),
        compiler_params=_cparams(("arbitrary",)),
        name="dsa_sample_attend",
    )(page_table.reshape(-1), q, k_new, v_new, mask, bias_last, bias_new, az, cache_k, cache_v)


def _final_kernel(x_ref, hm_ref, ha_ref, sgm_ref, sga_ref, wbm_ref, wba_ref, wo_ref, o_ref):
    ym = jnp.dot(hm_ref[...].astype(bf16), wbm_ref[...], preferred_element_type=f32)
    ya = jnp.dot(ha_ref[...].astype(bf16), wba_ref[...], preferred_element_type=f32)
    merged = sgm_ref[...] * ym + sga_ref[...] * ya
    o_ref[...] = x_ref[...] + jnp.dot(merged.astype(bf16), wo_ref[...], preferred_element_type=f32)


def _final(x, hm, ha, sg, wbm, wba, wo, tm):
    m, d = x.shape
    row = lambda w: pl.BlockSpec((tm, w), lambda i: (i, 0))
    full = lambda a: pl.BlockSpec(a.shape, lambda i: (0, 0))
    return pl.pallas_call(
        _final_kernel,
        out_shape=jax.ShapeDtypeStruct((m, d), f32),
        grid=(m // tm,),
        in_specs=[row(d), row(M_WIDTH), row(A_WIDTH),
                  pl.BlockSpec((tm, d), lambda i: (i, 0)), pl.BlockSpec((tm, d), lambda i: (i, 1)),
                  full(wbm), full(wba), full(wo)],
        out_specs=row(d),
        compiler_params=_cparams(("parallel",)),
        name="merge_out",
    )(x, hm, ha, sg, sg, wbm, wba, wo)


def _segments(w_in):
    widths = (("m_q", M_WIDTH), ("m_k", M_WIDTH), ("m_v", M_WIDTH), ("m_o", M_WIDTH), ("m_z", M_WIDTH),
              ("m_i", M_HEADS), ("m_f", M_HEADS),
              ("a_q", A_WIDTH), ("a_k", A_KV_WIDTH), ("a_v", A_KV_WIDTH), ("a_z", A_WIDTH),
              ("ix_q", IDX_HEADS * IDX_DIM), ("ix_k", IDX_DIM), ("ix_w", IDX_HEADS),
              ("g_m", w_in.shape[0]), ("g_a", w_in.shape[0]))
    seg, off = {}, 0
    for name, w in widths:
        seg[name] = w_in[:, off:off + w]
        off += w
    assert off == w_in.shape[1]
    return seg


def _rel_bucket(dist):
    max_exact = REL_BUCKETS // 2
    d = jnp.maximum(dist.astype(f32), 1.0)
    large = max_exact + (jnp.log(d / max_exact) / math.log(REL_MAX_DIST / max_exact)
                         * (REL_BUCKETS - max_exact)).astype(i32)
    large = jnp.minimum(large, REL_BUCKETS - 1)
    return jnp.where(dist < max_exact, dist, large)


def _bias_of(dist, rel_bias):
    far = rel_bias[REL_BUCKETS - 1]
    onehot = jax.nn.one_hot(_rel_bucket(jnp.maximum(dist, 0)), REL_BUCKETS, dtype=f32)
    return jnp.dot(onehot, rel_bias - far, precision=lax.Precision.HIGHEST)


def kernel(x_prompt, x_sample, cache_k, cache_v, cache_idx_k, state_C, state_n, state_m, page_table,
           rel_bias, w_norm, w_in, b_if, g_mlstm, g_q, g_k, w_branch_m, w_branch_a, w_out):
    assert w_in.shape[0] == 1, "single-layer trunk"
    bsz, seq, d = x_prompt.shape
    nreq, tokens, _ = x_sample.shape
    npages = page_table.shape[1]
    past = npages * PAGE_SIZE
    assert seq % LANE == 0 and 2 * tokens == SUBLANE and (nreq * tokens) % SUBLANE == 0
    assert npages % KV_CHUNK_PAGES == 0 and npages % SCORE_GROUP_PAGES == 0 and A_GROUP == 2

    rel_bias = rel_bias.astype(f32)
    seg = _segments(w_in[0].astype(f32))
    cast = lambda a: a.astype(bf16)
    w_qkv = cast(jnp.concatenate([seg["m_q"], seg["m_k"] * (M_DIM ** -0.5), seg["m_v"]], axis=1))
    w_small = cast(jnp.concatenate(
        [seg["ix_k"], seg["m_i"], seg["m_f"], seg["ix_w"],
         jnp.zeros((d, LANE - IDX_DIM - 2 * M_HEADS - IDX_HEADS), f32)], axis=1))
    w_g = cast(jnp.concatenate([seg["g_m"], seg["g_a"]], axis=1))
    w_o, w_z, w_az = cast(seg["m_o"]), cast(seg["m_z"]), cast(seg["a_z"])
    w_aq, w_ak, w_av, w_iq = cast(seg["a_q"]), cast(seg["a_k"]), cast(seg["a_v"]), cast(seg["ix_q"])
    gq_scaled = g_q[0].astype(f32) * (A_DIM ** -0.5)
    gk = g_k[0].astype(f32).reshape(1, A_DIM)
    bif_lane = jnp.zeros((1, LANE), f32).at[0, SM_I:SM_I + 2 * M_HEADS].set(b_if[0].astype(f32))
    wbm, wba, wo = cast(w_branch_m[0]), cast(w_branch_a[0]), cast(w_out[0])

    mp = bsz * seq
    tm = min(1024, mp)
    tn = 512
    xp = x_prompt.reshape(mp, d)
    xn = _rmsnorm(xp, w_norm[0], min(512, mp))
    (qkv,) = _proj_nat(_proj_cast_kernel, xn, [w_qkv], [], [bf16], tm, tn, "proj_mqkv")
    small, ik_bf = _proj_nat(_proj_dual_kernel, xn, [w_small], [], [f32, bf16], tm, LANE, "proj_small")
    (poz,) = _proj_nat(_proj_oz_kernel, xn, [w_o, w_z], [], [f32], tm, tn, "proj_moz")
    k_rows, k_bf = _proj_nat(_proj_knorm_kernel, xn, [w_ak], [gk], [f32, bf16], tm, tn, "proj_ak")
    (v_rows,) = _proj_nat(functools.partial(_proj_act_kernel, act=None), xn, [w_av], [], [f32],
                          tm, tn, "proj_av")
    (az,) = _proj_nat(functools.partial(_proj_act_kernel, act="silu"), xn, [w_az], [], [f32],
                      tm, tn, "proj_az")
    (sg,) = _proj_nat(functools.partial(_proj_act_kernel, act="sigmoid"), xn, [w_g], [], [f32],
                      tm, tn, "proj_gates")
    tmt = min(512, mp)
    gq_tile = jnp.broadcast_to(jnp.tile(gq_scaled, A_HEADS)[:, None], (A_WIDTH, LANE))
    qt = _proj_T(xn, w_aq.T, gq_tile, "qnorm", bf16, tmt, 512, "projT_aq")
    vt = _proj_T(xn, w_av.T, None, "cast", bf16, tmt, 512, "projT_av")
    iqt = _proj_T(xn, w_iq.T, None, "cast", bf16, tmt, IDX_HEADS * IDX_DIM, "projT_iq")
    smallt = _proj_T(xn, w_small.T, None, "cast", f32, tmt, LANE, "projT_small")

    chunk = 256 if seq % 256 == 0 else LANE
    hm, c_p, nm_p = _mlstm_prompt(qkv, small, poz, bif_lane, g_mlstm[0].astype(f32), bsz, seq, chunk)

    nsub = 2 if seq % (2 * LANE) == 0 else 1
    tb = nsub * LANE
    assert REL_MAX_DIST <= tb + 1
    tl = jnp.arange(LANE)
    tlb = jnp.arange(tb)
    dist0 = tlb[None, :] - tlb[:, None]
    bias_tiles = jnp.stack([_bias_of(dist0, rel_bias), _bias_of(dist0 + tb, rel_bias)])
    bias_tiles = jnp.transpose(bias_tiles, (0, 3, 1, 2))
    topk_p = min(TOPK_MAX, seq // 4)
    ha = _dsa_prompt(k_bf, vt, ik_bf, qt, iqt, smallt, bias_tiles, az, bsz, seq, topk_p, nsub)
    y_p = _final(xp, hm, ha, sg, wbm, wba, wo, min(512, mp))

    ms = nreq * tokens
    xs = x_sample.reshape(ms, d)
    tms = min(512, ms)
    xns = _rmsnorm(xs, w_norm[0], tms)
    nat = lambda body, ws, ex, outs, name, tn_=tn: _proj_nat(body, xns, ws, ex, outs, tms, tn_, name)
    (qkv_s,) = nat(_proj_cast_kernel, [w_qkv], [], [f32], "sproj_mqkv")
    (small_s,) = nat(functools.partial(_proj_act_kernel, act=None), [w_small], [], [f32], "sproj_small", LANE)
    (poz_s,) = nat(_proj_oz_kernel, [w_o, w_z], [], [f32], "sproj_moz")
    (q_s,) = nat(_proj_qnorm_kernel, [w_aq], [gq_scaled.reshape(1, A_DIM)], [f32], "sproj_aq")
    k_rows_s, _ = nat(_proj_knorm_kernel, [w_ak], [gk], [f32, bf16], "sproj_ak")
    (v_rows_s,) = nat(functools.partial(_proj_act_kernel, act=None), [w_av], [], [f32], "sproj_av")
    (az_s,) = nat(functools.partial(_proj_act_kernel, act="silu"), [w_az], [], [f32], "sproj_az")
    (sg_s,) = nat(functools.partial(_proj_act_kernel, act="sigmoid"), [w_g], [], [f32], "sproj_gates")
    (iq_s,) = nat(functools.partial(_proj_act_kernel, act=None), [w_iq], [], [f32], "sproj_iq",
                  IDX_HEADS * IDX_DIM)

    hm_s, c_s, n_s, m_s = _mlstm_sample(qkv_s, small_s, poz_s, bif_lane, g_mlstm[0].astype(f32),
                                        state_C[0].astype(f32), state_n[0].astype(f32),
                                        state_m[0].astype(f32), tokens)

    cache_ik_t = jnp.swapaxes(cache_idx_k[0].astype(f32), 1, 2)
    scores_s = _ds_scores(page_table, iq_s, small_s, cache_ik_t, tokens)
    topk_s = min(TOPK_MAX, (past + tokens) // 4)
    ntile = ms // SUBLANE
    mask_s = _ds_mask(scores_s, topk_s, math.gcd(ntile, 8))
    tq = jnp.arange(tokens)
    by_kv = lambda b: jnp.transpose(b, (2, 0, 1)).reshape(A_KV_HEADS, A_GROUP * tokens, LANE)
    dist_last = (past + tq)[:, None] - (past - PAGE_SIZE + tl)[None, :]
    chunk_keys = KV_CHUNK_PAGES * PAGE_SIZE
    bias_last = jnp.concatenate([jnp.zeros((A_KV_HEADS, A_GROUP * tokens, chunk_keys - PAGE_SIZE), f32),
                                 by_kv(_bias_of(dist_last, rel_bias))], axis=2)
    bias_new = by_kv(_bias_of(tq[:, None] - (tl % tokens)[None, :], rel_bias))
    o_s = _ds_attend(page_table, q_s, k_rows_s, v_rows_s, mask_s, bias_last, bias_new, az_s,
                     cache_k[0].astype(f32).reshape(-1, PAGE_SIZE * A_KV_HEADS, A_DIM),
                     cache_v[0].astype(f32).reshape(-1, PAGE_SIZE * A_KV_HEADS, A_DIM), tokens)
    y_s = _final(xs, hm_s, o_s, sg_s, wbm, wba, wo, tms)

    ck, cv, ci = cache_k.dtype, cache_v.dtype, cache_idx_k.dtype
    kv_shape_p = (1, bsz, seq, A_KV_HEADS, A_DIM)
    kv_shape_s = (1, nreq, tokens, A_KV_HEADS, A_DIM)
    return (y_p.reshape(bsz, seq, d).astype(x_prompt.dtype),
            y_s.reshape(nreq, tokens, d).astype(x_sample.dtype),
            k_rows.reshape(kv_shape_p).astype(ck),
            v_rows.reshape(kv_shape_p).astype(cv),
            small[:, SM_IK:SM_IK + IDX_DIM].reshape(1, bsz, seq, IDX_DIM).astype(ci),
            c_p.reshape(1, bsz, M_HEADS, M_DIM, M_DIM).astype(state_C.dtype),
            nm_p[:, :, 0, :].reshape(1, bsz, M_HEADS, M_DIM).astype(state_n.dtype),
            nm_p[:, :, 1, 0].reshape(1, bsz, M_HEADS).astype(state_m.dtype),
            k_rows_s.reshape(kv_shape_s).astype(ck),
            v_rows_s.reshape(kv_shape_s).astype(cv),
            small_s[:, SM_IK:SM_IK + IDX_DIM].reshape(1, nreq, tokens, IDX_DIM).astype(ci),
            c_s.reshape(1, nreq, M_HEADS, M_DIM, M_DIM).astype(state_C.dtype),
            n_s.reshape(1, nreq, M_HEADS, M_DIM).astype(state_n.dtype),
            m_s.reshape(1, nreq, M_HEADS).astype(state_m.dtype))
```

```python
import functools
import math

import jax
import jax.numpy as jnp
from jax import lax
from jax.experimental import pallas as pl
from jax.experimental.pallas import tpu as pltpu

M_HEADS = 4
M_DIM = 256
M_WIDTH = M_HEADS * M_DIM
A_HEADS = 8
A_KV_HEADS = 4
A_GROUP = A_HEADS // A_KV_HEADS
A_DIM = 128
A_WIDTH = A_HEADS * A_DIM
A_KV_WIDTH = A_KV_HEADS * A_DIM
IDX_HEADS = 4
IDX_DIM = 64
TOPK_MAX = 256
Q_BLOCK = 128
PAGE_SIZE = 128
REL_BUCKETS = 32
REL_MAX_DIST = 128
EPS = 1e-6
LOG2E = math.log2(math.e)

LANE = 128
SUBLANE = 8
VMEM_LIMIT = 56 * 1024 * 1024

SM_IK = 0
SM_I = IDX_DIM
SM_F = SM_I + M_HEADS
SM_W = SM_F + M_HEADS

f32 = jnp.float32
bf16 = jnp.bfloat16
i32 = jnp.int32

NEG_INF = float("-inf")
INT_MAX = 2147483647


def _cparams(sem):
    return pltpu.CompilerParams(dimension_semantics=sem, vmem_limit_bytes=VMEM_LIMIT)


def _sigmoid(x):
    return 1.0 / (1.0 + jnp.exp(-x))


def _log_sigmoid(x):
    return jnp.minimum(x, 0.0) - jnp.log(1.0 + jnp.exp(-jnp.abs(x)))


def _floor_avg(lo, hi):
    return (lo >> 1) + (hi >> 1) + (lo & hi & 1)


def _rmsnorm_kernel(x_ref, g_ref, o_ref):
    x = x_ref[...]
    ms = jnp.mean(x * x, axis=-1, keepdims=True)
    o_ref[...] = (x * lax.rsqrt(ms + EPS) * g_ref[...]).astype(bf16)


def _rmsnorm(x, g, tm):
    m, d = x.shape
    return pl.pallas_call(
        _rmsnorm_kernel,
        out_shape=jax.ShapeDtypeStruct((m, d), bf16),
        grid=(m // tm,),
        in_specs=[pl.BlockSpec((tm, d), lambda i: (i, 0)),
                  pl.BlockSpec((1, d), lambda i: (0, 0))],
        out_specs=pl.BlockSpec((tm, d), lambda i: (i, 0)),
        compiler_params=_cparams(("parallel",)),
        name="rmsnorm",
    )(x, g.reshape(1, d))


def _proj_cast_kernel(x_ref, w_ref, o_ref):
    o_ref[...] = jnp.dot(x_ref[...], w_ref[...], preferred_element_type=f32).astype(o_ref.dtype)


def _proj_dual_kernel(x_ref, w_ref, o_ref, ob_ref):
    a = jnp.dot(x_ref[...], w_ref[...], preferred_element_type=f32)
    o_ref[...] = a
    ob_ref[...] = a.astype(bf16)


def _proj_act_kernel(x_ref, w_ref, o_ref, *, act):
    a = jnp.dot(x_ref[...], w_ref[...], preferred_element_type=f32)
    if act == "silu":
        a = a * _sigmoid(a)
    elif act == "sigmoid":
        a = _sigmoid(a)
    o_ref[...] = a


def _proj_oz_kernel(x_ref, wo_ref, wz_ref, o_ref):
    x = x_ref[...]
    o = jnp.dot(x, wo_ref[...], preferred_element_type=f32)
    z = jnp.dot(x, wz_ref[...], preferred_element_type=f32)
    o_ref[...] = _sigmoid(o) * (z * _sigmoid(z))


def _proj_knorm_kernel(x_ref, w_ref, g_ref, o_ref, ob_ref):
    a = jnp.dot(x_ref[...], w_ref[...], preferred_element_type=f32)
    g = g_ref[...]
    for c in range(a.shape[1] // A_DIM):
        blk = a[:, c * A_DIM:(c + 1) * A_DIM]
        ms = jnp.mean(blk * blk, axis=-1, keepdims=True)
        kn = blk * lax.rsqrt(ms + EPS) * g
        o_ref[:, c * A_DIM:(c + 1) * A_DIM] = kn
        ob_ref[:, c * A_DIM:(c + 1) * A_DIM] = kn.astype(bf16)


def _proj_qnorm_kernel(x_ref, w_ref, g_ref, o_ref):
    a = jnp.dot(x_ref[...], w_ref[...], preferred_element_type=f32)
    g = g_ref[...]
    for c in range(a.shape[1] // A_DIM):
        blk = a[:, c * A_DIM:(c + 1) * A_DIM]
        ms = jnp.mean(blk * blk, axis=-1, keepdims=True)
        o_ref[:, c * A_DIM:(c + 1) * A_DIM] = (blk * lax.rsqrt(ms + EPS) * g).astype(o_ref.dtype)


def _proj_nat(body, xn, weights, extras, outs, tm, tn, name):
    m, d = xn.shape
    n = weights[0].shape[1]
    in_specs = [pl.BlockSpec((tm, d), lambda j, i: (i, 0))]
    in_specs += [pl.BlockSpec((d, tn), lambda j, i: (0, j)) for _ in weights]
    in_specs += [pl.BlockSpec((1, e.shape[1]), lambda j, i: (0, 0)) for e in extras]
    out_shape = [jax.ShapeDtypeStruct((m, n), dt) for dt in outs]
    out_specs = [pl.BlockSpec((tm, tn), lambda j, i: (i, j)) for _ in outs]
    res = pl.pallas_call(
        body,
        out_shape=out_shape,
        grid=(n // tn, m // tm),
        in_specs=in_specs,
        out_specs=out_specs,
        compiler_params=_cparams(("parallel", "parallel")),
        name=name,
    )(xn, *weights, *extras)
    return res


def _projT_kernel(x_ref, wt_ref, *rest, kind):
    if kind == "qnorm":
        g_ref, o_ref = rest
    else:
        (o_ref,) = rest
    a = lax.dot_general(wt_ref[...], x_ref[...], (((1,), (1,)), ((), ())),
                        preferred_element_type=f32)
    tn, tm = a.shape
    if kind == "qnorm":
        parts = []
        for c in range(tn // A_DIM):
            blk = a[c * A_DIM:(c + 1) * A_DIM, :]
            ms = jnp.mean(blk * blk, axis=0, keepdims=True)
            parts.append(blk * lax.rsqrt(ms + EPS))
        a = jnp.concatenate(parts, axis=0) if len(parts) > 1 else parts[0]
        g = g_ref[...]
    for c in range(tm // LANE):
        blk = a[:, c * LANE:(c + 1) * LANE]
        if kind == "qnorm":
            blk = blk * g
        o_ref[c] = blk.astype(o_ref.dtype)


def _proj_T(xn, wt, g, kind, out_dtype, tm, tn, name):
    m, d = xn.shape
    n = wt.shape[0]
    in_specs = [pl.BlockSpec((tm, d), lambda j, i: (i, 0)),
                pl.BlockSpec((tn, d), lambda j, i: (j, 0))]
    args = [xn, wt]
    if kind == "qnorm":
        in_specs.append(pl.BlockSpec((tn, LANE), lambda j, i: (j, 0)))
        args.append(g)
    return pl.pallas_call(
        functools.partial(_projT_kernel, kind=kind),
        out_shape=jax.ShapeDtypeStruct((m // LANE, n, LANE), out_dtype),
        grid=(n // tn, m // tm),
        in_specs=in_specs,
        out_specs=pl.BlockSpec((tm // LANE, tn, LANE), lambda j, i: (i, j, 0)),
        compiler_params=_cparams(("parallel", "parallel")),
        name=name,
    )(*args)


def _col_to_row(col, eye):
    return jnp.sum(jnp.where(eye, col, 0.0), axis=0, keepdims=True)


def _mlstm_head(q, k, v, ig_col, lf_col, c_prev, n_prev, m_prev, tri, eye):
    L = q.shape[0]
    ig_row = _col_to_row(ig_col, eye)
    lf_row = _col_to_row(lf_col, eye)
    b_col = jnp.sum(jnp.where(tri, lf_row, 0.0), axis=1, keepdims=True)
    b_row = _col_to_row(b_col, eye)
    dmat = jnp.where(tri, b_col - b_row + ig_row, NEG_INF)
    inter = b_col + m_prev
    m_t = jnp.maximum(inter, jnp.max(dmat, axis=1, keepdims=True))
    w_intra = jnp.exp(dmat - m_t)
    w_inter = jnp.exp(inter - m_t)
    s = lax.dot_general(q, k, (((1,), (1,)), ((), ())), preferred_element_type=f32) * w_intra
    num = w_inter * jnp.dot(q, c_prev.astype(bf16), preferred_element_type=f32)
    num = num + jnp.dot(s.astype(bf16), v, preferred_element_type=f32)
    den = w_inter * jnp.sum(q.astype(f32) * n_prev, axis=1, keepdims=True)
    den = den + jnp.sum(s, axis=1, keepdims=True)
    h = num / jnp.maximum(jnp.abs(den), jnp.exp(-m_t))
    b_last = b_row[:, L - 1:L]
    dec_row = b_last - b_row + ig_row
    m_new = jnp.maximum(b_last + m_prev, jnp.max(dec_row, axis=1, keepdims=True))
    wk_row = jnp.exp(dec_row - m_new)
    sc = jnp.exp(b_last + m_prev - m_new)
    wk_col = jnp.exp(b_last - b_col + ig_col - m_new)
    kw = (k.astype(f32) * wk_col).astype(bf16)
    c_new = sc * c_prev + lax.dot_general(kw, v, (((0,), (0,)), ((), ())),
                                          preferred_element_type=f32)
    wk8 = jnp.broadcast_to(wk_row, (SUBLANE, L)).astype(bf16)
    n_new = sc * n_prev + jnp.dot(wk8, k, preferred_element_type=f32)[0:1, :]
    return h, c_new, n_new, m_new


def _mlstm_prompt_kernel(qkv_q, qkv_k, qkv_v, sm_ref, poz_ref, bif_ref, g_ref,
                         hm_ref, c_out, nm_out, c_s, n_s, m_s):
    ci = pl.program_id(1)
    L = qkv_q.shape[0]

    @pl.when(ci == 0)
    def _():
        c_s[...] = jnp.zeros_like(c_s)
        n_s[...] = jnp.zeros_like(n_s)
        m_s[...] = jnp.zeros_like(m_s)

    r = lax.broadcasted_iota(i32, (L, L), 0)
    c = lax.broadcasted_iota(i32, (L, L), 1)
    tri = c <= r
    eye = c == r
    sm = sm_ref[...] + bif_ref[...]
    for h in range(M_HEADS):
        ig_col = sm[:, SM_I + h:SM_I + h + 1]
        lf_col = _log_sigmoid(sm[:, SM_F + h:SM_F + h + 1])
        sl = slice(h * M_DIM, (h + 1) * M_DIM)
        hh, c_new, n_new, m_new = _mlstm_head(
            qkv_q[:, sl], qkv_k[:, sl], qkv_v[:, sl], ig_col, lf_col,
            c_s[h], n_s[h][0:1, :], m_s[h][0:1, 0:1], tri, eye)
        c_s[h] = c_new
        n_s[h] = jnp.broadcast_to(n_new, (SUBLANE, M_DIM))
        m_s[h] = jnp.broadcast_to(m_new, (SUBLANE, LANE))
        ms = jnp.mean(hh * hh, axis=-1, keepdims=True)
        hn = hh * lax.rsqrt(ms + EPS) * g_ref[:, sl]
        hm_ref[:, sl] = (hn * poz_ref[:, sl]).astype(bf16)

    @pl.when(ci == pl.num_programs(1) - 1)
    def _():
        c_out[0] = c_s[...]
        for h in range(M_HEADS):
            nm_out[0, h] = jnp.concatenate(
                [n_s[h][0:1, :], jnp.broadcast_to(m_s[h][0:1, 0:1], (SUBLANE - 1, M_DIM))], axis=0)


def _mlstm_prompt(qkv, small, poz, bif_lane, g_mlstm, bsz, seq, L):
    nc = seq // L
    m = bsz * seq
    blk = lambda off: pl.BlockSpec((L, M_WIDTH), lambda b, c, off=off: (b * nc + c, off))
    return pl.pallas_call(
        _mlstm_prompt_kernel,
        out_shape=[jax.ShapeDtypeStruct((m, M_WIDTH), bf16),
                   jax.ShapeDtypeStruct((bsz, M_HEADS, M_DIM, M_DIM), f32),
                   jax.ShapeDtypeStruct((bsz, M_HEADS, SUBLANE, M_DIM), f32)],
        grid=(bsz, nc),
        in_specs=[blk(0), blk(1), blk(2),
                  pl.BlockSpec((L, LANE), lambda b, c: (b * nc + c, 0)),
                  pl.BlockSpec((L, M_WIDTH), lambda b, c: (b * nc + c, 0)),
                  pl.BlockSpec((1, LANE), lambda b, c: (0, 0)),
                  pl.BlockSpec((1, M_WIDTH), lambda b, c: (0, 0))],
        out_specs=[pl.BlockSpec((L, M_WIDTH), lambda b, c: (b * nc + c, 0)),
                   pl.BlockSpec((1, M_HEADS, M_DIM, M_DIM), lambda b, c: (b, 0, 0, 0)),
                   pl.BlockSpec((1, M_HEADS, SUBLANE, M_DIM), lambda b, c: (b, 0, 0, 0))],
        scratch_shapes=[pltpu.VMEM((M_HEADS, M_DIM, M_DIM), f32),
                        pltpu.VMEM((M_HEADS, SUBLANE, M_DIM), f32),
                        pltpu.VMEM((M_HEADS, SUBLANE, LANE), f32)],
        compiler_params=_cparams(("parallel", "arbitrary")),
        name="mlstm_prompt",
    )(qkv, qkv, qkv, small, poz, bif_lane, g_mlstm.reshape(1, M_WIDTH))


SAMPLE_PAD = 16


def _mlstm_sample_kernel(q_ref, k_ref, v_ref, sm_ref, poz_ref, bif_ref, g_ref, c_in, n_in, m_in,
                         hm_ref, c_out, n_out, m_out, *, tokens):
    L = SAMPLE_PAD
    rows = q_ref.shape[0]
    reqs = rows // tokens
    r = lax.broadcasted_iota(i32, (L, L), 0)
    c = lax.broadcasted_iota(i32, (L, L), 1)
    tri = c <= r
    eye = c == r
    row = lax.broadcasted_iota(i32, (L, 1), 0)
    row8 = lax.broadcasted_iota(i32, (rows, 1), 0)

    def pad(a):
        return jnp.concatenate([a, jnp.zeros((L - rows, a.shape[1]), a.dtype)], axis=0)

    sm = pad(sm_ref[...] + bif_ref[...])
    qp = pad(q_ref[...])
    kp = pad(k_ref[...])
    vp = pad(v_ref[...])
    for h in range(M_HEADS):
        sl = slice(h * M_DIM, (h + 1) * M_DIM)
        q = qp[:, sl].astype(bf16)
        k = kp[:, sl].astype(bf16)
        v = vp[:, sl].astype(bf16)
        h_tile = jnp.zeros((rows, M_DIM), f32)
        for rr in range(reqs):
            mine = (row >= rr * tokens) & (row < (rr + 1) * tokens)
            ig_col = jnp.where(mine, sm[:, SM_I + h:SM_I + h + 1], NEG_INF)
            lf_col = jnp.where(mine, _log_sigmoid(sm[:, SM_F + h:SM_F + h + 1]), 0.0)
            hh, c_new, n_new, m_new = _mlstm_head(
                q, k, v, ig_col, lf_col, c_in[rr, h], n_in[rr, h], m_in[rr, h], tri, eye)
            c_out[rr, h] = c_new
            n_out[rr, h] = n_new
            m_out[rr, h] = m_new
            mine8 = (row8 >= rr * tokens) & (row8 < (rr + 1) * tokens)
            h_tile = jnp.where(mine8, hh[0:rows, :], h_tile)
        ms = jnp.mean(h_tile * h_tile, axis=-1, keepdims=True)
        hn = h_tile * lax.rsqrt(ms + EPS) * g_ref[:, sl]
        hm_ref[:, sl] = hn * poz_ref[:, sl]


def _mlstm_sample(qkv, small, poz, bif_lane, g_mlstm, state_c, state_n, state_m, tokens):
    m = qkv.shape[0]
    rows = SUBLANE
    reqs = rows // tokens
    nreq = m // tokens
    blk = lambda off: pl.BlockSpec((rows, M_WIDTH), lambda i, off=off: (i, off))
    n4 = state_n.reshape(nreq, M_HEADS, 1, M_DIM)
    m4 = state_m.reshape(nreq, M_HEADS, 1, 1)
    return pl.pallas_call(
        functools.partial(_mlstm_sample_kernel, tokens=tokens),
        out_shape=[jax.ShapeDtypeStruct((m, M_WIDTH), f32),
                   jax.ShapeDtypeStruct(state_c.shape, f32),
                   jax.ShapeDtypeStruct(n4.shape, f32),
                   jax.ShapeDtypeStruct(m4.shape, f32)],
        grid=(m // rows,),
        in_specs=[blk(0), blk(1), blk(2),
                  pl.BlockSpec((rows, LANE), lambda i: (i, 0)),
                  pl.BlockSpec((rows, M_WIDTH), lambda i: (i, 0)),
                  pl.BlockSpec((1, LANE), lambda i: (0, 0)),
                  pl.BlockSpec((1, M_WIDTH), lambda i: (0, 0)),
                  pl.BlockSpec((reqs, M_HEADS, M_DIM, M_DIM), lambda i: (i, 0, 0, 0)),
                  pl.BlockSpec((reqs, M_HEADS, 1, M_DIM), lambda i: (i, 0, 0, 0)),
                  pl.BlockSpec((reqs, M_HEADS, 1, 1), lambda i: (i, 0, 0, 0))],
        out_specs=[pl.BlockSpec((rows, M_WIDTH), lambda i: (i, 0)),
                   pl.BlockSpec((reqs, M_HEADS, M_DIM, M_DIM), lambda i: (i, 0, 0, 0)),
                   pl.BlockSpec((reqs, M_HEADS, 1, M_DIM), lambda i: (i, 0, 0, 0)),
                   pl.BlockSpec((reqs, M_HEADS, 1, 1), lambda i: (i, 0, 0, 0))],
        compiler_params=_cparams(("parallel",)),
        name="mlstm_sample",
    )(qkv, qkv, qkv, small, poz, bif_lane, g_mlstm.reshape(1, M_WIDTH), state_c, n4, m4)


BISECT_STEPS = 64


def _key_mid(lo, hi):
    def key(x):
        b = lax.bitcast_convert_type(x, i32)
        return jnp.where(b < 0, b ^ INT_MAX, b)

    k = _floor_avg(key(lo), key(hi))
    return lax.bitcast_convert_type(jnp.where(k < 0, k ^ INT_MAX, k), f32)


def _kth_interval(count_ge, kk, vmin, vmax, active):
    margin = 2.0 ** -6
    lo0 = vmin - (jnp.abs(vmin) + 1.0) * margin
    hi0 = vmax + (jnp.abs(vmax) + 1.0) * margin

    def cond(carry):
        _, _, done, it = carry
        return (it < BISECT_STEPS) & (jnp.max(jnp.where(done > 0, 0.0, 1.0)) > 0.0)

    def body(carry):
        lo, hi, done, it = carry
        key_step = it % 2 == 1
        mid = jnp.where(key_step, _key_mid(lo, hi), 0.5 * lo + 0.5 * hi)
        inside = (mid > lo) & (mid < hi)
        cnt = count_ge(mid)
        ge = cnt >= kk
        live = (done == 0) & inside
        stop = (inside & (cnt == kk)) | (key_step & (mid == lo))
        lo = jnp.where(live & ge, mid, lo)
        hi = jnp.where(live & jnp.logical_not(ge), mid, hi)
        return lo, hi, jnp.where(stop, 1, done), it + 1

    done0 = jnp.where(active, 0, 1)
    lo, hi, _, _ = lax.while_loop(cond, body, (lo0, hi0, done0, jnp.int32(0)))
    return lo, hi


def _dsa_prompt_kernel(k_ref, vt_ref, ik_ref, qt_ref, iqt_ref, smt_ref, bias_ref, az_ref,
                       o_ref, sc_s, acc_s, m_s, l_s, *, topk, nsub):
    qi = pl.program_id(1)
    nblk = qi + 1
    TQ = nsub * LANE
    TK = nsub * LANE
    t_abs = qi * TQ + lax.broadcasted_iota(i32, (1, TQ), 1)
    s_loc = lax.broadcasted_iota(i32, (LANE, 1), 0)

    def lanes(ref3, base):
        return jnp.concatenate([ref3[base + u] for u in range(nsub)], axis=1)

    iqt = lanes(iqt_ref, 0)
    w_rows = lanes(smt_ref, 0)[SM_W:SM_W + SUBLANE, :]

    def col_sum(a):
        out = a[0:SUBLANE, :]
        for u in range(1, a.shape[0] // SUBLANE):
            out = out + a[u * SUBLANE:(u + 1) * SUBLANE, :]
        return out

    def score_blk(j, carry):
        vmax, vmin = carry
        for u in range(nsub):
            rows = pl.ds(j * TK + u * LANE, LANE)
            ikb = ik_ref[rows, :][:, SM_IK:SM_IK + IDX_DIM]
            acc = jnp.zeros((LANE, TQ), f32)
            for h in range(IDX_HEADS):
                s = jnp.dot(ikb, iqt[h * IDX_DIM:(h + 1) * IDX_DIM, :], preferred_element_type=f32)
                acc = acc + jnp.maximum(s, 0.0) * w_rows[h:h + 1, :]
            visible = j * TK + u * LANE + s_loc <= t_abs
            sc_s[rows, :] = jnp.where(visible, acc, NEG_INF)
            for g in range(LANE // SUBLANE):
                part = slice(g * SUBLANE, (g + 1) * SUBLANE)
                vmax = jnp.maximum(vmax, jnp.where(visible[part], acc[part], NEG_INF))
                vmin = jnp.minimum(vmin, jnp.where(visible[part], acc[part], -NEG_INF))
        return vmax, vmin

    vmax, vmin = lax.fori_loop(0, nblk, score_blk, (jnp.full((SUBLANE, TQ), NEG_INF, f32),
                                                    jnp.full((SUBLANE, TQ), -NEG_INF, f32)))
    vmax = jnp.max(vmax, axis=0, keepdims=True)
    vmin = jnp.min(vmin, axis=0, keepdims=True)

    def count_ge(th):
        def body(j, acc):
            return acc + col_sum(jnp.where(sc_s[pl.ds(j * TK, TK), :] >= th, 1.0, 0.0))
        acc = lax.fori_loop(0, nblk, body, jnp.zeros((SUBLANE, TQ), f32))
        return jnp.sum(acc, axis=0, keepdims=True)

    nvalid = (t_abs + 1).astype(f32)
    kk = jnp.minimum(float(topk), nvalid)
    lo, hi = _kth_interval(count_ge, kk, vmin, vmax, nvalid > float(topk))

    tie = count_ge(lo) > kk
    any_tie = jnp.max(jnp.where(tie, 1.0, 0.0))

    @pl.when(any_tie > 0.0)
    def _():
        need = kk - count_ge(hi)
        rr = lax.broadcasted_iota(i32, (LANE, LANE), 0)
        cc = lax.broadcasted_iota(i32, (LANE, LANE), 1)
        lower = jnp.where(cc < rr, 1.0, 0.0).astype(bf16)

        def body(j, run):
            rows = pl.ds(j * LANE, LANE)
            sc = sc_s[rows, :]
            eq = (sc >= lo) & (sc < hi)
            eqf = jnp.where(eq, 1.0, 0.0)
            rank = run + jnp.dot(lower, eqf.astype(bf16), preferred_element_type=f32)
            sc_s[rows, :] = jnp.where(eq & (rank >= need), NEG_INF, sc)
            return run + jnp.sum(eqf, axis=0, keepdims=True)

        lax.fori_loop(0, nblk * nsub, body, jnp.zeros((1, TQ), f32))

    def to_mask(j, _):
        rows = pl.ds(j * TK, TK)
        sc_s[rows, :] = jnp.where(sc_s[rows, :] >= lo, 0.0, NEG_INF)
        return 0

    lax.fori_loop(0, nblk, to_mask, 0)

    m_s[...] = jnp.full(m_s.shape, NEG_INF, f32)
    l_s[...] = jnp.zeros_like(l_s)
    acc_s[...] = jnp.zeros_like(acc_s)
    qt = lanes(qt_ref, 0)
    qg = [jnp.concatenate([qt[(A_GROUP * kv + g) * A_DIM:(A_GROUP * kv + g + 1) * A_DIM, :]
                           for g in range(A_GROUP)], axis=1) for kv in range(A_KV_HEADS)]

    def att_blk(j, near):
        rows = pl.ds(j * TK, TK)
        madd = sc_s[rows, :]
        madd = jnp.concatenate([madd] * A_GROUP, axis=1)
        kb = k_ref[rows, :]
        vtb = lanes(vt_ref, j * nsub)
        for kv in range(A_KV_HEADS):
            sl = slice(kv * A_DIM, (kv + 1) * A_DIM)
            lg = jnp.dot(kb[:, sl], qg[kv], preferred_element_type=f32)
            if near is not None:
                lg = lg + bias_ref[near, kv]
            lg = lg + madd
            m_old = m_s[kv][0:1, :]
            m_new = jnp.maximum(m_old, jnp.max(lg, axis=0, keepdims=True))
            m_safe = jnp.where(m_new == NEG_INF, 0.0, m_new)
            p = jnp.exp2(lg - m_safe)
            alpha = jnp.exp2(m_old - m_safe)
            l_new = alpha * l_s[kv][0:1, :] + jnp.sum(p, axis=0, keepdims=True)
            acc_s[kv] = alpha * acc_s[kv] + jnp.dot(vtb[sl, :], p.astype(bf16), preferred_element_type=f32)
            m_s[kv] = jnp.broadcast_to(m_new, (SUBLANE, A_GROUP * TQ))
            l_s[kv] = jnp.broadcast_to(l_new, (SUBLANE, A_GROUP * TQ))

    def far_blk(j, _):
        att_blk(j, None)
        return 0

    lax.fori_loop(0, qi - 1, far_blk, 0)

    @pl.when(qi >= 1)
    def _():
        att_blk(qi - 1, 1)

    att_blk(qi, 0)

    for h in range(A_HEADS):
        kv, g = h // A_GROUP, h % A_GROUP
        sl = slice(h * A_DIM, (h + 1) * A_DIM)
        cols = slice(g * TQ, (g + 1) * TQ)
        ot = acc_s[kv][:, cols] / l_s[kv][0:1, cols]
        o_ref[:, sl] = (ot.T * az_ref[:, sl]).astype(o_ref.dtype)


def _dsa_prompt(k_bf, vt, ik_bf, qt, iqt, smallt, bias_tiles, az, bsz, seq, topk, nsub):
    tb = nsub * LANE
    nq = seq // tb
    m = bsz * seq
    sub3 = lambda w: pl.BlockSpec((nsub, w, LANE), lambda b, i: (b * nq + i, 0, 0))
    return pl.pallas_call(
        functools.partial(_dsa_prompt_kernel, topk=topk, nsub=nsub),
        out_shape=jax.ShapeDtypeStruct((m, A_WIDTH), bf16),
        grid=(bsz, nq),
        in_specs=[pl.BlockSpec((seq, A_KV_WIDTH), lambda b, i: (b, 0)),
                  pl.BlockSpec((seq // LANE, A_KV_WIDTH, LANE), lambda b, i: (b, 0, 0)),
                  pl.BlockSpec((seq, LANE), lambda b, i: (b, 0)),
                  sub3(A_WIDTH), sub3(IDX_HEADS * IDX_DIM), sub3(LANE),
                  pl.BlockSpec(bias_tiles.shape, lambda b, i: (0, 0, 0, 0)),
                  pl.BlockSpec((tb, A_WIDTH), lambda b, i: (b * nq + i, 0))],
        out_specs=pl.BlockSpec((tb, A_WIDTH), lambda b, i: (b * nq + i, 0)),
        scratch_shapes=[pltpu.VMEM((seq, tb), f32),
                        pltpu.VMEM((A_KV_HEADS, A_DIM, A_GROUP * tb), f32),
                        pltpu.VMEM((A_KV_HEADS, SUBLANE, A_GROUP * tb), f32),
                        pltpu.VMEM((A_KV_HEADS, SUBLANE, A_GROUP * tb), f32)],
        compiler_params=_cparams(("parallel", "arbitrary")),
        name="dsa_prompt",
    )(k_bf, vt, ik_bf, qt, iqt, smallt, bias_tiles, az)


KV_CHUNK_PAGES = 8
SCORE_GROUP_PAGES = 16
MASK_WRITE_PAGES = 4


def _ds_scores_kernel(pt_ref, iq_ref, sm_ref, cache_ref, o_ref, ikbuf, sem, *, tokens, npages):
    pair = pl.program_id(0)
    rows = iq_ref.shape[0]
    reqs = rows // tokens
    sm = sm_ref[...]
    iq = iq_ref[...]
    iq_heads = jnp.concatenate([iq[:, h * IDX_DIM:(h + 1) * IDX_DIM] for h in range(IDX_HEADS)],
                               axis=0).astype(bf16)
    row = lax.broadcasted_iota(i32, (rows, 1), 0)

    def page_copy(step, par, rr, p):
        page = pt_ref[(step * reqs + rr) * npages + p]
        return pltpu.make_async_copy(cache_ref.at[page], ikbuf.at[par, rr, p], sem.at[par, rr])

    def issue_all(step, par):
        for rr in range(reqs):
            def issue(p2, _, rr=rr):
                page_copy(step, par, rr, 2 * p2).start(priority=0)
                page_copy(step, par, rr, 2 * p2 + 1).start(priority=1)
                return 0
            lax.fori_loop(0, npages // 2, issue, 0)

    def head_sum(s):
        acc = jnp.zeros((rows, s.shape[1]), f32)
        for h in range(IDX_HEADS):
            acc = acc + jnp.maximum(s[h * rows:(h + 1) * rows, :], 0.0) * sm[:, SM_W + h:SM_W + h + 1]
        return acc

    par = pair % 2

    @pl.when(pair == 0)
    def _():
        issue_all(pair, par)

    @pl.when(pair + 1 < pl.num_programs(0))
    def _():
        issue_all(pair + 1, 1 - par)

    for rr in range(reqs):
        def drain(p, _, rr=rr):
            page_copy(pair, par, rr, p).wait()
            return 0
        lax.fori_loop(0, npages, drain, 0)

    req_of_row = row // tokens

    def group_scores(gi, _):
        p0 = gi * SCORE_GROUP_PAGES
        sc = None
        for rr in range(reqs):
            keys = jnp.concatenate([ikbuf[par, rr, p0 + j] for j in range(SCORE_GROUP_PAGES)], axis=1)
            sc_rr = head_sum(jnp.dot(iq_heads, keys.astype(bf16), preferred_element_type=f32))
            sc = sc_rr if sc is None else jnp.where(req_of_row == rr, sc_rr, sc)
        for j in range(SCORE_GROUP_PAGES):
            o_ref[0, p0 + j] = sc[:, j * PAGE_SIZE:(j + 1) * PAGE_SIZE]
        return 0

    lax.fori_loop(0, npages // SCORE_GROUP_PAGES, group_scores, 0)

    ik_new = jnp.concatenate([sm[:, SM_IK:SM_IK + IDX_DIM],
                              jnp.zeros((LANE - rows, IDX_DIM), f32)], axis=0).astype(bf16)
    s = lax.dot_general(iq_heads, ik_new, (((1,), (1,)), ((), ())), preferred_element_type=f32)
    col = lax.broadcasted_iota(i32, (1, LANE), 1)
    ok = (col < rows) & ((col // tokens) == (row // tokens)) & (col <= row)
    o_ref[0, npages] = jnp.where(ok, head_sum(s), NEG_INF)


def _ds_scores(page_table, iq, small, cache_ik_t, tokens):
    m = iq.shape[0]
    nreq, npages = page_table.shape
    rows = SUBLANE
    reqs = rows // tokens
    return pl.pallas_call(
        functools.partial(_ds_scores_kernel, tokens=tokens, npages=npages),
        out_shape=jax.ShapeDtypeStruct((m // rows, npages + 1, rows, LANE), f32),
        grid_spec=pltpu.PrefetchScalarGridSpec(
            num_scalar_prefetch=1,
            grid=(m // rows,),
            in_specs=[pl.BlockSpec((rows, IDX_HEADS * IDX_DIM), lambda i, pt: (i, 0)),
                      pl.BlockSpec((rows, LANE), lambda i, pt: (i, 0)),
                      pl.BlockSpec(memory_space=pl.ANY)],
            out_specs=pl.BlockSpec((1, npages + 1, rows, LANE), lambda i, pt: (i, 0, 0, 0)),
            scratch_shapes=[pltpu.VMEM((2, reqs, npages, IDX_DIM, PAGE_SIZE), f32),
                            pltpu.SemaphoreType.DMA((2, reqs))]),
        compiler_params=_cparams(("arbitrary",)),
        name="dsa_sample_scores",
    )(page_table.reshape(-1), iq, small, cache_ik_t)


def _ds_mask_kernel(sc_ref, o_ref, sc_s, *, topk):
    tb, p1 = sc_ref.shape[0], sc_ref.shape[1]
    sc_s[...] = sc_ref[...]

    def row_reduce(a, op):
        return op(op(a, axis=1, keepdims=True), axis=3, keepdims=True)

    def count_ge(th):
        return row_reduce(jnp.where(sc_s[...] >= th, 1.0, 0.0), jnp.sum)

    sc = sc_s[...]
    visible = sc > NEG_INF
    nvalid = row_reduce(jnp.where(visible, 1.0, 0.0), jnp.sum)
    vmax = row_reduce(sc, jnp.max)
    vmin = row_reduce(jnp.where(visible, sc, -NEG_INF), jnp.min)
    kk = jnp.minimum(float(topk), nvalid)
    lo, hi = _kth_interval(count_ge, kk, vmin, vmax, nvalid > float(topk))
    tie = count_ge(lo) > kk
    any_tie = jnp.max(jnp.where(tie, 1.0, 0.0))

    @pl.when(any_tie > 0.0)
    def _():
        need = kk - count_ge(hi)
        rr = lax.broadcasted_iota(i32, (LANE, LANE), 0)
        cc = lax.broadcasted_iota(i32, (LANE, LANE), 1)
        upper = jnp.where(rr < cc, 1.0, 0.0).astype(bf16)

        def page(p, run):
            new = []
            for t in range(tb):
                blk = sc_s[t, p]
                eq = (blk >= lo[t, 0]) & (blk < hi[t, 0])
                eqf = jnp.where(eq, 1.0, 0.0)
                rank = run[t] + jnp.dot(eqf.astype(bf16), upper, preferred_element_type=f32)
                sc_s[t, p] = jnp.where(eq & (rank >= need[t, 0]), NEG_INF, blk)
                new.append(run[t] + jnp.sum(eqf, axis=1, keepdims=True))
            return tuple(new)

        lax.fori_loop(0, p1, page, tuple(jnp.zeros((SUBLANE, 1), f32) for _ in range(tb)))

    slot = lax.broadcasted_iota(i32, (LANE, LANE * A_KV_HEADS), 0)
    lane = lax.broadcasted_iota(i32, (LANE, LANE * A_KV_HEADS), 1)
    spread = jnp.where(lane // A_KV_HEADS == slot, 1.0, 0.0).astype(bf16)

    def write_page(p):
        sel = jnp.concatenate([jnp.where(sc_s[t, p] >= lo[t, 0], 1.0, 0.0) for t in range(tb)], axis=0)
        wide = jnp.dot(sel.astype(bf16), spread, preferred_element_type=f32)
        wide = jnp.where(wide > 0.5, 0.0, NEG_INF)
        for t in range(tb):
            o_ref[t, p] = wide[t * SUBLANE:(t + 1) * SUBLANE, :]

    def write_group(gi, _):
        for j in range(MASK_WRITE_PAGES):
            write_page(gi * MASK_WRITE_PAGES + j)
        return 0

    ngroup = p1 // MASK_WRITE_PAGES
    lax.fori_loop(0, ngroup, write_group, 0)
    for p in range(ngroup * MASK_WRITE_PAGES, p1):
        write_page(p)


def _ds_mask(scores, topk, tb):
    nt, p1, rows, lanes = scores.shape
    blk = pl.BlockSpec((tb, p1, rows, lanes), lambda i: (i, 0, 0, 0))
    return pl.pallas_call(
        functools.partial(_ds_mask_kernel, topk=topk),
        out_shape=jax.ShapeDtypeStruct((nt, p1, rows, lanes * A_KV_HEADS), f32),
        grid=(nt // tb,),
        in_specs=[blk],
        out_specs=pl.BlockSpec((tb, p1, rows, lanes * A_KV_HEADS), lambda i: (i, 0, 0, 0)),
        scratch_shapes=[pltpu.VMEM((tb, p1, rows, lanes), f32)],
        compiler_params=_cparams(("parallel",)),
        name="dsa_sample_mask",
    )(scores)


def _ds_attend_kernel(pt_ref, q_ref, knew_ref, vnew_ref, mask_ref, biasl_ref, biasn_ref, az_ref,
                      kc_ref, vc_ref, o_ref, kbuf, vbuf, sem, *, tokens, npages):
    pair = pl.program_id(0)
    rows = q_ref.shape[0]
    reqs = rows // tokens
    nrow = A_HEADS * rows
    nchunk = npages // KV_CHUNK_PAGES
    total = reqs * nchunk
    page_rows = PAGE_SIZE * A_KV_HEADS
    rowi = lax.broadcasted_iota(i32, (nrow, 1), 0)
    lane = lax.broadcasted_iota(i32, (1, LANE), 1)
    kv_match = (lane % A_KV_HEADS) == (rowi // rows) // A_GROUP
    req_of_row = (rowi % rows) // tokens

    def chunk_copies(g, slot):
        cps = []
        for p in range(KV_CHUNK_PAGES):
            page = pt_ref[g * KV_CHUNK_PAGES + p]
            dst = pl.ds(p * page_rows, page_rows)
            cps.append(pltpu.make_async_copy(kc_ref.at[page], kbuf.at[slot, dst], sem.at[slot, 0]))
            cps.append(pltpu.make_async_copy(vc_ref.at[page], vbuf.at[slot, dst], sem.at[slot, 1]))
        return cps

    def start_chunk(g, slot):
        for n, cp in enumerate(chunk_copies(g, slot)):
            cp.start(priority=n % 2)

    assert total % 2 == 0
    first = pair * total

    @pl.when(pair == 0)
    def _():
        start_chunk(0, 0)

    def update(state, lg, vb):
        m_old, l_old, acc = state
        m_new = jnp.maximum(m_old, jnp.max(lg, axis=1, keepdims=True))
        m_safe = jnp.where(m_new == NEG_INF, 0.0, m_new)
        p = jnp.exp(lg - m_safe)
        alpha = jnp.exp(m_old - m_safe)
        l_new = alpha * l_old + jnp.sum(p, axis=1, keepdims=True)
        acc = alpha * acc + jnp.dot(p.astype(bf16), vb, preferred_element_type=f32)
        return m_new, l_new, acc

    def pad_rows(a):
        return jnp.concatenate([a, jnp.zeros((LANE - a.shape[0], a.shape[1]), a.dtype)], axis=0).astype(bf16)

    q = q_ref[...]
    q_rows = jnp.concatenate([q[:, h * A_DIM:(h + 1) * A_DIM] for h in range(A_HEADS)],
                             axis=0).astype(bf16)

    def tile_rows(a):
        return jnp.concatenate([a] * A_HEADS, axis=0)

    state = (jnp.full((nrow, 1), NEG_INF, f32), jnp.zeros((nrow, 1), f32), jnp.zeros((nrow, A_DIM), f32))
    for rr in range(reqs):
        pat = jnp.where(kv_match & (req_of_row == rr), 0.0, NEG_INF)
        pat = jnp.concatenate([pat] * (page_rows // LANE), axis=1)

        def chunk_step(c, state, bias, rr=rr, pat=pat):
            g = first + rr * nchunk + c
            slot = g % 2

            @pl.when(g + 1 < pl.num_programs(0) * total)
            def _():
                start_chunk(g + 1, 1 - slot)

            for cp in chunk_copies(g, slot):
                cp.wait()
            lg = lax.dot_general(q_rows, kbuf[slot].astype(bf16), (((1,), (1,)), ((), ())),
                                 preferred_element_type=f32)
            parts = []
            for p in range(KV_CHUNK_PAGES):
                blk = lg[:, p * page_rows:(p + 1) * page_rows]
                blk = blk + (tile_rows(mask_ref[0, c * KV_CHUNK_PAGES + p]) + pat)
                if bias is not None and p == KV_CHUNK_PAGES - 1:
                    blk = blk + bias
                parts.append(blk)
            return update(state, jnp.concatenate(parts, axis=1), vbuf[slot].astype(bf16))

        state = lax.fori_loop(0, nchunk - 1, functools.partial(chunk_step, bias=None), state)
        state = chunk_step(nchunk - 1, state, biasl_ref[...])

    lg = lax.dot_general(q_rows, pad_rows(knew_ref[...]), (((1,), (1,)), ((), ())),
                         preferred_element_type=f32)
    madd = tile_rows(mask_ref[0, npages][:, 0:LANE]) + jnp.where(kv_match, 0.0, NEG_INF)
    _, l_fin, acc = update(state, lg + biasn_ref[...] + madd, pad_rows(vnew_ref[...]))
    o = acc / l_fin
    for h in range(A_HEADS):
        sl = slice(h * A_DIM, (h + 1) * A_DIM)
        o_ref[:, sl] = o[h * rows:(h + 1) * rows, :] * az_ref[:, sl]


def _ds_attend(page_table, q, k_new, v_new, mask, bias_last, bias_new, az, cache_k, cache_v, tokens):
    nreq, npages = page_table.shape
    m = q.shape[0]
    rows = SUBLANE
    chunk = KV_CHUNK_PAGES * PAGE_SIZE
    tile = lambda w: pl.BlockSpec((rows, w), lambda i, pt: (i, 0))
    const2 = lambda a: pl.BlockSpec(a.shape, lambda i, pt: (0, 0))
    new_rows = pl.BlockSpec((rows * A_KV_HEADS, A_DIM), lambda i, pt: (i, 0))
    return pl.pallas_call(
        functools.partial(_ds_attend_kernel, tokens=tokens, npages=npages),
        out_shape=jax.ShapeDtypeStruct((m, A_WIDTH), f32),
        grid_spec=pltpu.PrefetchScalarGridSpec(
            num_scalar_prefetch=1,
            grid=(m // rows,),
            in_specs=[tile(A_WIDTH), new_rows, new_rows,
                      pl.BlockSpec((1, npages + 1, rows, LANE * A_KV_HEADS), lambda i, pt: (i, 0, 0, 0)),
                      const2(bias_last), const2(bias_new), tile(A_WIDTH),
                      pl.BlockSpec(memory_space=pl.ANY),
                      pl.BlockSpec(memory_space=pl.ANY)],
            out_specs=tile(A_WIDTH),
            scratch_shapes=[pltpu.VMEM((2, chunk * A_KV_HEADS, A_DIM), f32),
                            pltpu.VMEM((2, chunk * A_KV_HEADS, A_DIM), f32),
                            pltpu.SemaphoreType.DMA((2, 2))]),
        compiler_params=_cparams(("arbitrary",)),
        name="dsa_sample_attend",
    )(page_table.reshape(-1), q, k_new, v_new, mask, bias_last, bias_new, az, cache_k, cache_v)


def _final_kernel(x_ref, hm_ref, ha_ref, sgm_ref, sga_ref, wbm_ref, wba_ref, wo_ref, o_ref):
    ym = jnp.dot(hm_ref[...].astype(bf16), wbm_ref[...], preferred_element_type=f32)
    ya = jnp.dot(ha_ref[...].astype(bf16), wba_ref[...], preferred_element_type=f32)
    merged = sgm_ref[...] * ym + sga_ref[...] * ya
    o_ref[...] = x_ref[...] + jnp.dot(merged.astype(bf16), wo_ref[...], preferred_element_type=f32)


def _final(x, hm, ha, sg, wbm, wba, wo, tm):
    m, d = x.shape
    row = lambda w: pl.BlockSpec((tm, w), lambda i: (i, 0))
    full = lambda a: pl.BlockSpec(a.shape, lambda i: (0, 0))
    return pl.pallas_call(
        _final_kernel,
        out_shape=jax.ShapeDtypeStruct((m, d), f32),
        grid=(m // tm,),
        in_specs=[row(d), row(M_WIDTH), row(A_WIDTH),
                  pl.BlockSpec((tm, d), lambda i: (i, 0)), pl.BlockSpec((tm, d), lambda i: (i, 1)),
                  full(wbm), full(wba), full(wo)],
        out_specs=row(d),
        compiler_params=_cparams(("parallel",)),
        name="merge_out",
    )(x, hm, ha, sg, sg, wbm, wba, wo)


def _segments(w_in):
    widths = (("m_q", M_WIDTH), ("m_k", M_WIDTH), ("m_v", M_WIDTH), ("m_o", M_WIDTH), ("m_z", M_WIDTH),
              ("m_i", M_HEADS), ("m_f", M_HEADS),
              ("a_q", A_WIDTH), ("a_k", A_KV_WIDTH), ("a_v", A_KV_WIDTH), ("a_z", A_WIDTH),
              ("ix_q", IDX_HEADS * IDX_DIM), ("ix_k", IDX_DIM), ("ix_w", IDX_HEADS),
              ("g_m", w_in.shape[0]), ("g_a", w_in.shape[0]))
    seg, off = {}, 0
    for name, w in widths:
        seg[name] = w_in[:, off:off + w]
        off += w
    assert off == w_in.shape[1]
    return seg


def _rel_bucket(dist):
    max_exact = REL_BUCKETS // 2
    d = jnp.maximum(dist.astype(f32), 1.0)
    large = max_exact + jnp.floor(jnp.log(d / max_exact) / math.log(REL_MAX_DIST / max_exact)
                                  * (REL_BUCKETS - max_exact)).astype(i32)
    large = jnp.minimum(large, REL_BUCKETS - 1)
    return jnp.where(dist < max_exact, dist, large)


def _bias_of(dist, rel_bias):
    far = rel_bias[REL_BUCKETS - 1]
    onehot = jax.nn.one_hot(_rel_bucket(jnp.maximum(dist, 0)), REL_BUCKETS, dtype=f32)
    return jnp.dot(onehot, rel_bias - far, precision=lax.Precision.HIGHEST)


def kernel(x_prompt, x_sample, cache_k, cache_v, cache_idx_k, state_C, state_n, state_m, page_table,
           rel_bias, w_norm, w_in, b_if, g_mlstm, g_q, g_k, w_branch_m, w_branch_a, w_out):
    assert w_in.shape[0] == 1, "single-layer trunk"
    bsz, seq, d = x_prompt.shape
    nreq, tokens, _ = x_sample.shape
    npages = page_table.shape[1]
    past = npages * PAGE_SIZE
    assert seq % LANE == 0 and 2 * tokens == SUBLANE and (nreq * tokens) % SUBLANE == 0
    assert npages % KV_CHUNK_PAGES == 0 and npages % SCORE_GROUP_PAGES == 0 and A_GROUP == 2

    rel_bias = rel_bias.astype(f32)
    seg = _segments(w_in[0].astype(f32))
    cast = lambda a: a.astype(bf16)
    w_qkv = cast(jnp.concatenate([seg["m_q"], seg["m_k"] * (M_DIM ** -0.5), seg["m_v"]], axis=1))
    w_small = cast(jnp.concatenate(
        [seg["ix_k"], seg["m_i"], seg["m_f"], seg["ix_w"],
         jnp.zeros((d, LANE - IDX_DIM - 2 * M_HEADS - IDX_HEADS), f32)], axis=1))
    w_g = cast(jnp.concatenate([seg["g_m"], seg["g_a"]], axis=1))
    w_o, w_z, w_az = cast(seg["m_o"]), cast(seg["m_z"]), cast(seg["a_z"])
    w_aq, w_ak, w_av, w_iq = cast(seg["a_q"]), cast(seg["a_k"]), cast(seg["a_v"]), cast(seg["ix_q"])
    gq_scaled = g_q[0].astype(f32) * (A_DIM ** -0.5)
    gk = g_k[0].astype(f32).reshape(1, A_DIM)
    bif_lane = jnp.zeros((1, LANE), f32).at[0, SM_I:SM_I + 2 * M_HEADS].set(b_if[0].astype(f32))
    wbm, wba, wo = cast(w_branch_m[0]), cast(w_branch_a[0]), cast(w_out[0])

    mp = bsz * seq
    tm = min(1024, mp)
    tn = 512
    xp = x_prompt.reshape(mp, d)
    xn = _rmsnorm(xp, w_norm[0], min(512, mp))
    (qkv,) = _proj_nat(_proj_cast_kernel, xn, [w_qkv], [], [bf16], tm, tn, "proj_mqkv")
    small, ik_bf = _proj_nat(_proj_dual_kernel, xn, [w_small], [], [f32, bf16], tm, LANE, "proj_small")
    (poz,) = _proj_nat(_proj_oz_kernel, xn, [w_o, w_z], [], [f32], tm, tn, "proj_moz")
    k_rows, k_bf = _proj_nat(_proj_knorm_kernel, xn, [w_ak], [gk], [f32, bf16], tm, tn, "proj_ak")
    (v_rows,) = _proj_nat(functools.partial(_proj_act_kernel, act=None), xn, [w_av], [], [f32],
                          tm, tn, "proj_av")
    (az,) = _proj_nat(functools.partial(_proj_act_kernel, act="silu"), xn, [w_az], [], [f32],
                      tm, tn, "proj_az")
    (sg,) = _proj_nat(functools.partial(_proj_act_kernel, act="sigmoid"), xn, [w_g], [], [f32],
                      tm, tn, "proj_gates")
    tmt = min(512, mp)
    gq_tile = jnp.broadcast_to(jnp.tile(gq_scaled * LOG2E, A_HEADS)[:, None], (A_WIDTH, LANE))
    qt = _proj_T(xn, w_aq.T, gq_tile, "qnorm", bf16, tmt, 512, "projT_aq")
    vt = _proj_T(xn, w_av.T, None, "cast", bf16, tmt, 512, "projT_av")
    iqt = _proj_T(xn, w_iq.T, None, "cast", bf16, tmt, IDX_HEADS * IDX_DIM, "projT_iq")
    smallt = _proj_T(xn, w_small.T, None, "cast", f32, tmt, LANE, "projT_small")

    chunk = 256 if seq % 256 == 0 else LANE
    hm, c_p, nm_p = _mlstm_prompt(qkv, small, poz, bif_lane, g_mlstm[0].astype(f32), bsz, seq, chunk)

    nsub = 2 if seq % (2 * LANE) == 0 else 1
    tb = nsub * LANE
    assert REL_MAX_DIST <= tb + 1
    tl = jnp.arange(LANE)
    tlb = jnp.arange(tb)
    dist0 = tlb[None, :] - tlb[:, None]
    bias_tiles = jnp.stack([_bias_of(dist0, rel_bias), _bias_of(dist0 + tb, rel_bias)]) * LOG2E
    bias_tiles = bias_tiles.reshape(2, tb, tb, A_KV_HEADS, A_GROUP)
    bias_tiles = jnp.transpose(bias_tiles, (0, 3, 1, 4, 2)).reshape(2, A_KV_HEADS, tb, A_GROUP * tb)
    topk_p = min(TOPK_MAX, seq // 4)
    ha = _dsa_prompt(k_bf, vt, ik_bf, qt, iqt, smallt, bias_tiles, az, bsz, seq, topk_p, nsub)
    y_p = _final(xp, hm, ha, sg, wbm, wba, wo, min(512, mp))

    ms = nreq * tokens
    xs = x_sample.reshape(ms, d)
    tms = min(512, ms)
    xns = _rmsnorm(xs, w_norm[0], tms)
    nat = lambda body, ws, ex, outs, name, tn_=tn: _proj_nat(body, xns, ws, ex, outs, tms, tn_, name)
    (qkv_s,) = nat(_proj_cast_kernel, [w_qkv], [], [f32], "sproj_mqkv")
    (small_s,) = nat(functools.partial(_proj_act_kernel, act=None), [w_small], [], [f32], "sproj_small", LANE)
    (poz_s,) = nat(_proj_oz_kernel, [w_o, w_z], [], [f32], "sproj_moz")
    (q_s,) = nat(_proj_qnorm_kernel, [w_aq], [gq_scaled.reshape(1, A_DIM)], [f32], "sproj_aq")
    k_rows_s, _ = nat(_proj_knorm_kernel, [w_ak], [gk], [f32, bf16], "sproj_ak")
    (v_rows_s,) = nat(functools.partial(_proj_act_kernel, act=None), [w_av], [], [f32], "sproj_av")
    (az_s,) = nat(functools.partial(_proj_act_kernel, act="silu"), [w_az], [], [f32], "sproj_az")
    (sg_s,) = nat(functools.partial(_proj_act_kernel, act="sigmoid"), [w_g], [], [f32], "sproj_gates")
    (iq_s,) = nat(functools.partial(_proj_act_kernel, act=None), [w_iq], [], [f32], "sproj_iq",
                  IDX_HEADS * IDX_DIM)

    hm_s, c_s, n_s, m_s = _mlstm_sample(qkv_s, small_s, poz_s, bif_lane, g_mlstm[0].astype(f32),
                                        state_C[0].astype(f32), state_n[0].astype(f32),
                                        state_m[0].astype(f32), tokens)

    cache_ik_t = jnp.swapaxes(cache_idx_k[0].astype(f32), 1, 2)
    scores_s = _ds_scores(page_table, iq_s, small_s, cache_ik_t, tokens)
    topk_s = min(TOPK_MAX, (past + tokens) // 4)
    ntile = ms // SUBLANE
    mask_s = _ds_mask(scores_s, topk_s, math.gcd(ntile, 8))
    tok_of_row = jnp.arange(SUBLANE) % tokens

    def logit_layout(b):
        b = jnp.transpose(b, (2, 0, 1))[..., None]
        b = jnp.broadcast_to(b, b.shape[:3] + (A_KV_HEADS,))
        return b.reshape(A_HEADS * SUBLANE, -1)

    dist_last = (PAGE_SIZE + tok_of_row)[:, None] - tl[None, :]
    bias_last = logit_layout(_bias_of(dist_last, rel_bias))
    dist_new = tok_of_row[:, None] - tok_of_row[None, :]
    bias_new = logit_layout(_bias_of(dist_new, rel_bias))
    bias_new = jnp.pad(bias_new, ((0, 0), (0, LANE - bias_new.shape[1])))
    flat_rows = lambda a: a.reshape(ms * A_KV_HEADS, A_DIM)
    o_s = _ds_attend(page_table, q_s, flat_rows(k_rows_s), flat_rows(v_rows_s), mask_s, bias_last,
                     bias_new, az_s,
                     cache_k[0].astype(f32).reshape(-1, PAGE_SIZE * A_KV_HEADS, A_DIM),
                     cache_v[0].astype(f32).reshape(-1, PAGE_SIZE * A_KV_HEADS, A_DIM), tokens)
    y_s = _final(xs, hm_s, o_s, sg_s, wbm, wba, wo, tms)

    ck, cv, ci = cache_k.dtype, cache_v.dtype, cache_idx_k.dtype
    kv_shape_p = (1, bsz, seq, A_KV_HEADS, A_DIM)
    kv_shape_s = (1, nreq, tokens, A_KV_HEADS, A_DIM)
    return (y_p.reshape(bsz, seq, d).astype(x_prompt.dtype),
            y_s.reshape(nreq, tokens, d).astype(x_sample.dtype),
            k_rows.reshape(kv_shape_p).astype(ck),
            v_rows.reshape(kv_shape_p).astype(cv),
            small[:, SM_IK:SM_IK + IDX_DIM].reshape(1, bsz, seq, IDX_DIM).astype(ci),
            c_p.reshape(1, bsz, M_HEADS, M_DIM, M_DIM).astype(state_C.dtype),
            nm_p[:, :, 0, :].reshape(1, bsz, M_HEADS, M_DIM).astype(state_n.dtype),
            nm_p[:, :, 1, 0].reshape(1, bsz, M_HEADS).astype(state_m.dtype),
            k_rows_s.reshape(kv_shape_s).astype(ck),
            v_rows_s.reshape(kv_shape_s).astype(cv),
            small_s[:, SM_IK:SM_IK + IDX_DIM].reshape(1, nreq, tokens, IDX_DIM).astype(ci),
            c_s.reshape(1, nreq, M_HEADS, M_DIM, M_DIM).astype(state_C.dtype),
            n_s.reshape(1, nreq, M_HEADS, M_DIM).astype(state_n.dtype),
            m_s.reshape(1, nreq, M_HEADS).astype(state_m.dtype))
```

```python
import functools
import math

import jax
import jax.numpy as jnp
from jax import lax
from jax.experimental import pallas as pl
from jax.experimental.pallas import tpu as pltpu

M_HEADS = 4
M_DIM = 256
M_WIDTH = M_HEADS * M_DIM
A_HEADS = 8
A_KV_HEADS = 4
A_GROUP = A_HEADS // A_KV_HEADS
A_DIM = 128
A_WIDTH = A_HEADS * A_DIM
A_KV_WIDTH = A_KV_HEADS * A_DIM
IDX_HEADS = 4
IDX_DIM = 64
TOPK_MAX = 256
Q_BLOCK = 128
PAGE_SIZE = 128
REL_BUCKETS = 32
REL_MAX_DIST = 128
EPS = 1e-6
LOG2E = math.log2(math.e)

LANE = 128
SUBLANE = 8
VMEM_LIMIT = 56 * 1024 * 1024

SM_IK = 0
SM_I = IDX_DIM
SM_F = SM_I + M_HEADS
SM_W = SM_F + M_HEADS

f32 = jnp.float32
bf16 = jnp.bfloat16
i32 = jnp.int32

NEG_INF = float("-inf")
INT_MAX = 2147483647


def _cparams(sem):
    return pltpu.CompilerParams(dimension_semantics=sem, vmem_limit_bytes=VMEM_LIMIT)


def _sigmoid(x):
    return 1.0 / (1.0 + jnp.exp(-x))


def _log_sigmoid(x):
    return jnp.minimum(x, 0.0) - jnp.log(1.0 + jnp.exp(-jnp.abs(x)))


def _floor_avg(lo, hi):
    return (lo >> 1) + (hi >> 1) + (lo & hi & 1)


def _rmsnorm_kernel(x_ref, g_ref, o_ref):
    x = x_ref[...]
    ms = jnp.mean(x * x, axis=-1, keepdims=True)
    o_ref[...] = (x * lax.rsqrt(ms + EPS) * g_ref[...]).astype(bf16)


def _rmsnorm(x, g, tm):
    m, d = x.shape
    return pl.pallas_call(
        _rmsnorm_kernel,
        out_shape=jax.ShapeDtypeStruct((m, d), bf16),
        grid=(m // tm,),
        in_specs=[pl.BlockSpec((tm, d), lambda i: (i, 0)),
                  pl.BlockSpec((1, d), lambda i: (0, 0))],
        out_specs=pl.BlockSpec((tm, d), lambda i: (i, 0)),
        compiler_params=_cparams(("parallel",)),
        name="rmsnorm",
    )(x, g.reshape(1, d))


def _xwt(x, wt):
    return lax.dot_general(x, wt, (((1,), (1,)), ((), ())), preferred_element_type=f32)


def _proj_cast_kernel(x_ref, w_ref, o_ref):
    o_ref[...] = _xwt(x_ref[...], w_ref[...]).astype(o_ref.dtype)


def _proj_dual_kernel(x_ref, w_ref, o_ref, ob_ref):
    a = _xwt(x_ref[...], w_ref[...])
    o_ref[...] = a
    ob_ref[...] = a.astype(bf16)


def _proj_act_kernel(x_ref, w_ref, o_ref, *, act):
    a = _xwt(x_ref[...], w_ref[...])
    if act == "silu":
        a = a * _sigmoid(a)
    elif act == "sigmoid":
        a = _sigmoid(a)
    o_ref[...] = a


def _proj_rows_kernel(x_ref, w_ref, o_ref):
    a = _xwt(x_ref[...], w_ref[...])
    for c in range(a.shape[1] // A_DIM):
        o_ref[:, c, :] = a[:, c * A_DIM:(c + 1) * A_DIM]


def _proj_oz_kernel(x_ref, wo_ref, wz_ref, o_ref):
    x = x_ref[...]
    o = _xwt(x, wo_ref[...])
    z = _xwt(x, wz_ref[...])
    o_ref[...] = _sigmoid(o) * (z * _sigmoid(z))


def _proj_knorm_kernel(x_ref, w_ref, g_ref, o_ref, ob_ref):
    a = _xwt(x_ref[...], w_ref[...])
    g = g_ref[...]
    for c in range(a.shape[1] // A_DIM):
        blk = a[:, c * A_DIM:(c + 1) * A_DIM]
        ms = jnp.mean(blk * blk, axis=-1, keepdims=True)
        kn = blk * lax.rsqrt(ms + EPS) * g
        o_ref[:, c, :] = kn
        ob_ref[:, c * A_DIM:(c + 1) * A_DIM] = kn.astype(bf16)


def _proj_qnorm_kernel(x_ref, w_ref, g_ref, o_ref):
    a = _xwt(x_ref[...], w_ref[...])
    g = g_ref[...]
    for c in range(a.shape[1] // A_DIM):
        blk = a[:, c * A_DIM:(c + 1) * A_DIM]
        ms = jnp.mean(blk * blk, axis=-1, keepdims=True)
        o_ref[:, c * A_DIM:(c + 1) * A_DIM] = (blk * lax.rsqrt(ms + EPS) * g).astype(o_ref.dtype)


def _proj_nat(body, xn, weights, extras, outs, tm, tn, name, rows3d=()):
    m, d = xn.shape
    n = weights[0].shape[0]
    in_specs = [pl.BlockSpec((tm, d), lambda j, i: (i, 0))]
    in_specs += [pl.BlockSpec((tn, d), lambda j, i: (j, 0)) for _ in weights]
    in_specs += [pl.BlockSpec((1, e.shape[1]), lambda j, i: (0, 0)) for e in extras]
    out_shape, out_specs = [], []
    for k, dt in enumerate(outs):
        if k in rows3d:
            out_shape.append(jax.ShapeDtypeStruct((m, n // A_DIM, A_DIM), dt))
            out_specs.append(pl.BlockSpec((tm, tn // A_DIM, A_DIM), lambda j, i: (i, j, 0)))
        else:
            out_shape.append(jax.ShapeDtypeStruct((m, n), dt))
            out_specs.append(pl.BlockSpec((tm, tn), lambda j, i: (i, j)))
    res = pl.pallas_call(
        body,
        out_shape=out_shape,
        grid=(n // tn, m // tm),
        in_specs=in_specs,
        out_specs=out_specs,
        compiler_params=_cparams(("parallel", "parallel")),
        name=name,
    )(xn, *weights, *extras)
    return res


def _projT_kernel(x_ref, wt_ref, *rest, kind):
    if kind == "qnorm":
        g_ref, o_ref = rest
    else:
        (o_ref,) = rest
    a = lax.dot_general(wt_ref[...], x_ref[...], (((1,), (1,)), ((), ())),
                        preferred_element_type=f32)
    tn, tm = a.shape
    if kind == "qnorm":
        parts = []
        for c in range(tn // A_DIM):
            blk = a[c * A_DIM:(c + 1) * A_DIM, :]
            ms = jnp.mean(blk * blk, axis=0, keepdims=True)
            parts.append(blk * lax.rsqrt(ms + EPS))
        a = jnp.concatenate(parts, axis=0) if len(parts) > 1 else parts[0]
        g = g_ref[...]
    for c in range(tm // LANE):
        blk = a[:, c * LANE:(c + 1) * LANE]
        if kind == "qnorm":
            blk = blk * g
        o_ref[c] = blk.astype(o_ref.dtype)


def _proj_T(xn, wt, g, kind, out_dtype, tm, tn, name):
    m, d = xn.shape
    n = wt.shape[0]
    in_specs = [pl.BlockSpec((tm, d), lambda j, i: (i, 0)),
                pl.BlockSpec((tn, d), lambda j, i: (j, 0))]
    args = [xn, wt]
    if kind == "qnorm":
        in_specs.append(pl.BlockSpec((tn, LANE), lambda j, i: (j, 0)))
        args.append(g)
    return pl.pallas_call(
        functools.partial(_projT_kernel, kind=kind),
        out_shape=jax.ShapeDtypeStruct((m // LANE, n, LANE), out_dtype),
        grid=(n // tn, m // tm),
        in_specs=in_specs,
        out_specs=pl.BlockSpec((tm // LANE, tn, LANE), lambda j, i: (i, j, 0)),
        compiler_params=_cparams(("parallel", "parallel")),
        name=name,
    )(*args)


def _col_to_row(col, eye):
    return jnp.sum(jnp.where(eye, col, 0.0), axis=0, keepdims=True)


def _mlstm_head(q, k, v, ig_col, lf_col, c_prev, n_prev, m_prev, tri, eye):
    L = q.shape[0]
    ig_row = _col_to_row(ig_col, eye)
    lf_row = _col_to_row(lf_col, eye)
    b_col = jnp.sum(jnp.where(tri, lf_row, 0.0), axis=1, keepdims=True)
    b_row = _col_to_row(b_col, eye)
    dmat = jnp.where(tri, b_col - b_row + ig_row, NEG_INF)
    inter = b_col + m_prev
    m_t = jnp.maximum(inter, jnp.max(dmat, axis=1, keepdims=True))
    w_intra = jnp.exp(dmat - m_t)
    w_inter = jnp.exp(inter - m_t)
    s = lax.dot_general(q, k, (((1,), (1,)), ((), ())), preferred_element_type=f32) * w_intra
    num = w_inter * jnp.dot(q, c_prev.astype(bf16), preferred_element_type=f32)
    num = num + jnp.dot(s.astype(bf16), v, preferred_element_type=f32)
    den = w_inter * jnp.sum(q.astype(f32) * n_prev, axis=1, keepdims=True)
    den = den + jnp.sum(s, axis=1, keepdims=True)
    h = num / jnp.maximum(jnp.abs(den), jnp.exp(-m_t))
    b_last = b_row[:, L - 1:L]
    dec_row = b_last - b_row + ig_row
    m_new = jnp.maximum(b_last + m_prev, jnp.max(dec_row, axis=1, keepdims=True))
    wk_row = jnp.exp(dec_row - m_new)
    sc = jnp.exp(b_last + m_prev - m_new)
    wk_col = jnp.exp(b_last - b_col + ig_col - m_new)
    kw = (k.astype(f32) * wk_col).astype(bf16)
    c_new = sc * c_prev + lax.dot_general(kw, v, (((0,), (0,)), ((), ())),
                                          preferred_element_type=f32)
    wk8 = jnp.broadcast_to(wk_row, (SUBLANE, L)).astype(bf16)
    n_new = sc * n_prev + jnp.dot(wk8, k, preferred_element_type=f32)[0:1, :]
    return h, c_new, n_new, m_new


def _mlstm_prompt_kernel(qkv_q, qkv_k, qkv_v, sm_ref, poz_ref, bif_ref, g_ref,
                         hm_ref, c_out, nm_out, c_s, n_s, m_s):
    ci = pl.program_id(1)
    L = qkv_q.shape[0]

    @pl.when(ci == 0)
    def _():
        c_s[...] = jnp.zeros_like(c_s)
        n_s[...] = jnp.zeros_like(n_s)
        m_s[...] = jnp.zeros_like(m_s)

    r = lax.broadcasted_iota(i32, (L, L), 0)
    c = lax.broadcasted_iota(i32, (L, L), 1)
    tri = c <= r
    eye = c == r
    sm = sm_ref[...] + bif_ref[...]
    for h in range(M_HEADS):
        ig_col = sm[:, SM_I + h:SM_I + h + 1]
        lf_col = _log_sigmoid(sm[:, SM_F + h:SM_F + h + 1])
        sl = slice(h * M_DIM, (h + 1) * M_DIM)
        hh, c_new, n_new, m_new = _mlstm_head(
            qkv_q[:, sl], qkv_k[:, sl], qkv_v[:, sl], ig_col, lf_col,
            c_s[h], n_s[h][0:1, :], m_s[h][0:1, 0:1], tri, eye)
        c_s[h] = c_new
        n_s[h] = jnp.broadcast_to(n_new, (SUBLANE, M_DIM))
        m_s[h] = jnp.broadcast_to(m_new, (SUBLANE, LANE))
        ms = jnp.mean(hh * hh, axis=-1, keepdims=True)
        hn = hh * lax.rsqrt(ms + EPS) * g_ref[:, sl]
        hm_ref[:, sl] = (hn * poz_ref[:, sl]).astype(bf16)

    @pl.when(ci == pl.num_programs(1) - 1)
    def _():
        c_out[0] = c_s[...]
        for h in range(M_HEADS):
            nm_out[0, h] = jnp.concatenate(
                [n_s[h][0:1, :], jnp.broadcast_to(m_s[h][0:1, 0:1], (SUBLANE - 1, M_DIM))], axis=0)


def _mlstm_prompt(qkv, small, poz, bif_lane, g_mlstm, bsz, seq, L):
    nc = seq // L
    m = bsz * seq
    blk = lambda off: pl.BlockSpec((L, M_WIDTH), lambda b, c, off=off: (b * nc + c, off))
    return pl.pallas_call(
        _mlstm_prompt_kernel,
        out_shape=[jax.ShapeDtypeStruct((m, M_WIDTH), bf16),
                   jax.ShapeDtypeStruct((bsz, M_HEADS, M_DIM, M_DIM), f32),
                   jax.ShapeDtypeStruct((bsz, M_HEADS, SUBLANE, M_DIM), f32)],
        grid=(bsz, nc),
        in_specs=[blk(0), blk(1), blk(2),
                  pl.BlockSpec((L, LANE), lambda b, c: (b * nc + c, 0)),
                  pl.BlockSpec((L, M_WIDTH), lambda b, c: (b * nc + c, 0)),
                  pl.BlockSpec((1, LANE), lambda b, c: (0, 0)),
                  pl.BlockSpec((1, M_WIDTH), lambda b, c: (0, 0))],
        out_specs=[pl.BlockSpec((L, M_WIDTH), lambda b, c: (b * nc + c, 0)),
                   pl.BlockSpec((1, M_HEADS, M_DIM, M_DIM), lambda b, c: (b, 0, 0, 0)),
                   pl.BlockSpec((1, M_HEADS, SUBLANE, M_DIM), lambda b, c: (b, 0, 0, 0))],
        scratch_shapes=[pltpu.VMEM((M_HEADS, M_DIM, M_DIM), f32),
                        pltpu.VMEM((M_HEADS, SUBLANE, M_DIM), f32),
                        pltpu.VMEM((M_HEADS, SUBLANE, LANE), f32)],
        compiler_params=_cparams(("parallel", "arbitrary")),
        name="mlstm_prompt",
    )(qkv, qkv, qkv, small, poz, bif_lane, g_mlstm.reshape(1, M_WIDTH))


SAMPLE_PAD = 16


def _mlstm_sample_kernel(q_ref, k_ref, v_ref, sm_ref, poz_ref, bif_ref, g_ref, c_in, n_in, m_in,
                         hm_ref, c_out, n_out, m_out, *, tokens):
    L = SAMPLE_PAD
    rows = q_ref.shape[0]
    reqs = rows // tokens
    r = lax.broadcasted_iota(i32, (L, L), 0)
    c = lax.broadcasted_iota(i32, (L, L), 1)
    tri = c <= r
    eye = c == r
    row = lax.broadcasted_iota(i32, (L, 1), 0)
    row8 = lax.broadcasted_iota(i32, (rows, 1), 0)

    def pad(a):
        return jnp.concatenate([a, jnp.zeros((L - rows, a.shape[1]), a.dtype)], axis=0)

    sm = pad(sm_ref[...] + bif_ref[...])
    qp = pad(q_ref[...])
    kp = pad(k_ref[...])
    vp = pad(v_ref[...])
    for h in range(M_HEADS):
        sl = slice(h * M_DIM, (h + 1) * M_DIM)
        q = qp[:, sl].astype(bf16)
        k = kp[:, sl].astype(bf16)
        v = vp[:, sl].astype(bf16)
        h_tile = jnp.zeros((rows, M_DIM), f32)
        for rr in range(reqs):
            mine = (row >= rr * tokens) & (row < (rr + 1) * tokens)
            ig_col = jnp.where(mine, sm[:, SM_I + h:SM_I + h + 1], NEG_INF)
            lf_col = jnp.where(mine, _log_sigmoid(sm[:, SM_F + h:SM_F + h + 1]), 0.0)
            hh, c_new, n_new, m_new = _mlstm_head(
                q, k, v, ig_col, lf_col, c_in[rr, h], n_in[rr, h], m_in[rr, h], tri, eye)
            c_out[rr, h] = c_new
            n_out[rr, h] = n_new
            m_out[rr, h] = m_new
            mine8 = (row8 >= rr * tokens) & (row8 < (rr + 1) * tokens)
            h_tile = jnp.where(mine8, hh[0:rows, :], h_tile)
        ms = jnp.mean(h_tile * h_tile, axis=-1, keepdims=True)
        hn = h_tile * lax.rsqrt(ms + EPS) * g_ref[:, sl]
        hm_ref[:, sl] = hn * poz_ref[:, sl]


def _mlstm_sample(qkv, small, poz, bif_lane, g_mlstm, state_c, state_n, state_m, tokens):
    m = qkv.shape[0]
    rows = SUBLANE
    reqs = rows // tokens
    nreq = m // tokens
    blk = lambda off: pl.BlockSpec((rows, M_WIDTH), lambda i, off=off: (i, off))
    n4 = state_n.reshape(nreq, M_HEADS, 1, M_DIM)
    m4 = state_m.reshape(nreq, M_HEADS, 1, 1)
    return pl.pallas_call(
        functools.partial(_mlstm_sample_kernel, tokens=tokens),
        out_shape=[jax.ShapeDtypeStruct((m, M_WIDTH), f32),
                   jax.ShapeDtypeStruct(state_c.shape, f32),
                   jax.ShapeDtypeStruct(n4.shape, f32),
                   jax.ShapeDtypeStruct(m4.shape, f32)],
        grid=(m // rows,),
        in_specs=[blk(0), blk(1), blk(2),
                  pl.BlockSpec((rows, LANE), lambda i: (i, 0)),
                  pl.BlockSpec((rows, M_WIDTH), lambda i: (i, 0)),
                  pl.BlockSpec((1, LANE), lambda i: (0, 0)),
                  pl.BlockSpec((1, M_WIDTH), lambda i: (0, 0)),
                  pl.BlockSpec((reqs, M_HEADS, M_DIM, M_DIM), lambda i: (i, 0, 0, 0)),
                  pl.BlockSpec((reqs, M_HEADS, 1, M_DIM), lambda i: (i, 0, 0, 0)),
                  pl.BlockSpec((reqs, M_HEADS, 1, 1), lambda i: (i, 0, 0, 0))],
        out_specs=[pl.BlockSpec((rows, M_WIDTH), lambda i: (i, 0)),
                   pl.BlockSpec((reqs, M_HEADS, M_DIM, M_DIM), lambda i: (i, 0, 0, 0)),
                   pl.BlockSpec((reqs, M_HEADS, 1, M_DIM), lambda i: (i, 0, 0, 0)),
                   pl.BlockSpec((reqs, M_HEADS, 1, 1), lambda i: (i, 0, 0, 0))],
        compiler_params=_cparams(("parallel",)),
        name="mlstm_sample",
    )(qkv, qkv, qkv, small, poz, bif_lane, g_mlstm.reshape(1, M_WIDTH), state_c, n4, m4)


BISECT_STEPS = 64


def _key_mid(lo, hi):
    def key(x):
        b = lax.bitcast_convert_type(x, i32)
        return jnp.where(b < 0, b ^ INT_MAX, b)

    k = _floor_avg(key(lo), key(hi))
    return lax.bitcast_convert_type(jnp.where(k < 0, k ^ INT_MAX, k), f32)


def _kth_interval(count_ge, kk, vmin, vmax, active):
    margin = 2.0 ** -6
    lo0 = vmin - (jnp.abs(vmin) + 1.0) * margin
    hi0 = vmax + (jnp.abs(vmax) + 1.0) * margin

    def cond(carry):
        _, _, done, it = carry
        return (it < BISECT_STEPS) & (jnp.max(jnp.where(done > 0, 0.0, 1.0)) > 0.0)

    def body(carry):
        lo, hi, done, it = carry
        key_step = it % 2 == 1
        mid = jnp.where(key_step, _key_mid(lo, hi), 0.5 * lo + 0.5 * hi)
        inside = (mid > lo) & (mid < hi)
        cnt = count_ge(mid)
        ge = cnt >= kk
        live = (done == 0) & inside
        stop = (inside & (cnt == kk)) | (key_step & (mid == lo))
        lo = jnp.where(live & ge, mid, lo)
        hi = jnp.where(live & jnp.logical_not(ge), mid, hi)
        return lo, hi, jnp.where(stop, 1, done), it + 1

    done0 = jnp.where(active, 0, 1)
    lo, hi, _, _ = lax.while_loop(cond, body, (lo0, hi0, done0, jnp.int32(0)))
    return lo, hi


def _dsa_prompt_kernel(k_ref, vt_ref, ik_ref, qt_ref, iqt_ref, smt_ref, bias_ref, az_ref,
                       o_ref, sc_s, acc_s, m_s, l_s, *, topk, nsub):
    qi = pl.program_id(1)
    nblk = qi + 1
    TQ = nsub * LANE
    TK = nsub * LANE
    t_abs = qi * TQ + lax.broadcasted_iota(i32, (1, TQ), 1)
    s_loc = lax.broadcasted_iota(i32, (LANE, 1), 0)

    def lanes(ref3, base):
        return jnp.concatenate([ref3[base + u] for u in range(nsub)], axis=1)

    iqt = lanes(iqt_ref, 0)
    w_rows = lanes(smt_ref, 0)[SM_W:SM_W + SUBLANE, :]

    def col_sum(a):
        out = a[0:SUBLANE, :]
        for u in range(1, a.shape[0] // SUBLANE):
            out = out + a[u * SUBLANE:(u + 1) * SUBLANE, :]
        return out

    def score_blk(j, carry):
        vmax, vmin = carry
        for u in range(nsub):
            rows = pl.ds(j * TK + u * LANE, LANE)
            ikb = ik_ref[rows, :][:, SM_IK:SM_IK + IDX_DIM]
            acc = jnp.zeros((LANE, TQ), f32)
            for h in range(IDX_HEADS):
                s = jnp.dot(ikb, iqt[h * IDX_DIM:(h + 1) * IDX_DIM, :], preferred_element_type=f32)
                acc = acc + jnp.maximum(s, 0.0) * w_rows[h:h + 1, :]
            visible = j * TK + u * LANE + s_loc <= t_abs
            sc_s[rows, :] = jnp.where(visible, acc, NEG_INF)
            for g in range(LANE // SUBLANE):
                part = slice(g * SUBLANE, (g + 1) * SUBLANE)
                vmax = jnp.maximum(vmax, jnp.where(visible[part], acc[part], NEG_INF))
                vmin = jnp.minimum(vmin, jnp.where(visible[part], acc[part], -NEG_INF))
        return vmax, vmin

    vmax, vmin = lax.fori_loop(0, nblk, score_blk, (jnp.full((SUBLANE, TQ), NEG_INF, f32),
                                                    jnp.full((SUBLANE, TQ), -NEG_INF, f32)))
    vmax = jnp.max(vmax, axis=0, keepdims=True)
    vmin = jnp.min(vmin, axis=0, keepdims=True)

    def count_ge(th):
        def body(j, acc):
            return acc + col_sum(jnp.where(sc_s[pl.ds(j * TK, TK), :] >= th, 1.0, 0.0))
        acc = lax.fori_loop(0, nblk, body, jnp.zeros((SUBLANE, TQ), f32))
        return jnp.sum(acc, axis=0, keepdims=True)

    nvalid = (t_abs + 1).astype(f32)
    kk = jnp.minimum(float(topk), nvalid)
    lo, hi = _kth_interval(count_ge, kk, vmin, vmax, nvalid > float(topk))

    tie = count_ge(lo) > kk
    any_tie = jnp.max(jnp.where(tie, 1.0, 0.0))

    @pl.when(any_tie > 0.0)
    def _():
        need = kk - count_ge(hi)
        rr = lax.broadcasted_iota(i32, (LANE, LANE), 0)
        cc = lax.broadcasted_iota(i32, (LANE, LANE), 1)
        lower = jnp.where(cc < rr, 1.0, 0.0).astype(bf16)

        def body(j, run):
            rows = pl.ds(j * LANE, LANE)
            sc = sc_s[rows, :]
            eq = (sc >= lo) & (sc < hi)
            eqf = jnp.where(eq, 1.0, 0.0)
            rank = run + jnp.dot(lower, eqf.astype(bf16), preferred_element_type=f32)
            sc_s[rows, :] = jnp.where(eq & (rank >= need), NEG_INF, sc)
            return run + jnp.sum(eqf, axis=0, keepdims=True)

        lax.fori_loop(0, nblk * nsub, body, jnp.zeros((1, TQ), f32))

    def to_mask(j, _):
        rows = pl.ds(j * TK, TK)
        sc_s[rows, :] = jnp.where(sc_s[rows, :] >= lo, 0.0, NEG_INF)
        return 0

    lax.fori_loop(0, nblk, to_mask, 0)

    m_s[...] = jnp.full(m_s.shape, NEG_INF, f32)
    l_s[...] = jnp.zeros_like(l_s)
    acc_s[...] = jnp.zeros_like(acc_s)
    qt = lanes(qt_ref, 0)
    qg = [jnp.concatenate([qt[(A_GROUP * kv + g) * A_DIM:(A_GROUP * kv + g + 1) * A_DIM, :]
                           for g in range(A_GROUP)], axis=1) for kv in range(A_KV_HEADS)]

    def att_blk(j, near):
        rows = pl.ds(j * TK, TK)
        madd = sc_s[rows, :]
        madd = jnp.concatenate([madd] * A_GROUP, axis=1)
        kb = k_ref[rows, :]
        vtb = lanes(vt_ref, j * nsub)
        for kv in range(A_KV_HEADS):
            sl = slice(kv * A_DIM, (kv + 1) * A_DIM)
            lg = jnp.dot(kb[:, sl], qg[kv], preferred_element_type=f32)
            if near is not None:
                lg = lg + bias_ref[near, kv]
            lg = lg + madd
            m_old = m_s[kv][0:1, :]
            m_new = jnp.maximum(m_old, jnp.max(lg, axis=0, keepdims=True))
            m_safe = jnp.where(m_new == NEG_INF, 0.0, m_new)
            p = jnp.exp2(lg - m_safe)
            alpha = jnp.exp2(m_old - m_safe)
            l_new = alpha * l_s[kv][0:1, :] + jnp.sum(p, axis=0, keepdims=True)
            acc_s[kv] = alpha * acc_s[kv] + jnp.dot(vtb[sl, :], p.astype(bf16), preferred_element_type=f32)
            m_s[kv] = jnp.broadcast_to(m_new, (SUBLANE, A_GROUP * TQ))
            l_s[kv] = jnp.broadcast_to(l_new, (SUBLANE, A_GROUP * TQ))

    def far_blk(j, _):
        att_blk(j, None)
        return 0

    lax.fori_loop(0, qi - 1, far_blk, 0)

    @pl.when(qi >= 1)
    def _():
        att_blk(qi - 1, 1)

    att_blk(qi, 0)

    for h in range(A_HEADS):
        kv, g = h // A_GROUP, h % A_GROUP
        sl = slice(h * A_DIM, (h + 1) * A_DIM)
        cols = slice(g * TQ, (g + 1) * TQ)
        ot = acc_s[kv][:, cols] / l_s[kv][0:1, cols]
        o_ref[:, sl] = (ot.T * az_ref[:, sl]).astype(o_ref.dtype)


def _dsa_prompt(k_bf, vt, ik_bf, qt, iqt, smallt, bias_tiles, az, bsz, seq, topk, nsub):
    tb = nsub * LANE
    nq = seq // tb
    m = bsz * seq
    sub3 = lambda w: pl.BlockSpec((nsub, w, LANE), lambda b, i: (b * nq + i, 0, 0))
    return pl.pallas_call(
        functools.partial(_dsa_prompt_kernel, topk=topk, nsub=nsub),
        out_shape=jax.ShapeDtypeStruct((m, A_WIDTH), bf16),
        grid=(bsz, nq),
        in_specs=[pl.BlockSpec((seq, A_KV_WIDTH), lambda b, i: (b, 0)),
                  pl.BlockSpec((seq // LANE, A_KV_WIDTH, LANE), lambda b, i: (b, 0, 0)),
                  pl.BlockSpec((seq, LANE), lambda b, i: (b, 0)),
                  sub3(A_WIDTH), sub3(IDX_HEADS * IDX_DIM), sub3(LANE),
                  pl.BlockSpec(bias_tiles.shape, lambda b, i: (0, 0, 0, 0)),
                  pl.BlockSpec((tb, A_WIDTH), lambda b, i: (b * nq + i, 0))],
        out_specs=pl.BlockSpec((tb, A_WIDTH), lambda b, i: (b * nq + i, 0)),
        scratch_shapes=[pltpu.VMEM((seq, tb), f32),
                        pltpu.VMEM((A_KV_HEADS, A_DIM, A_GROUP * tb), f32),
                        pltpu.VMEM((A_KV_HEADS, SUBLANE, A_GROUP * tb), f32),
                        pltpu.VMEM((A_KV_HEADS, SUBLANE, A_GROUP * tb), f32)],
        compiler_params=_cparams(("parallel", "arbitrary")),
        name="dsa_prompt",
    )(k_bf, vt, ik_bf, qt, iqt, smallt, bias_tiles, az)


KV_CHUNK_PAGES = 8
SCORE_GROUP_PAGES = 16
MASK_WRITE_PAGES = 4
KV_BUFFERS = 3


def _ds_scores_kernel(pt_ref, iq_ref, sm_ref, cache_ref, o_ref, ikbuf, sem, *, tokens, npages):
    pair = pl.program_id(0)
    rows = iq_ref.shape[0]
    reqs = rows // tokens
    sm = sm_ref[...]
    iq = iq_ref[...]
    iq_heads = jnp.concatenate([iq[:, h * IDX_DIM:(h + 1) * IDX_DIM] for h in range(IDX_HEADS)],
                               axis=0).astype(bf16)
    row = lax.broadcasted_iota(i32, (rows, 1), 0)

    def page_copy(step, par, rr, p):
        page = pt_ref[(step * reqs + rr) * npages + p]
        return pltpu.make_async_copy(cache_ref.at[page], ikbuf.at[par, rr, p], sem.at[par, rr])

    def issue_all(step, par):
        for rr in range(reqs):
            def issue(p2, _, rr=rr):
                page_copy(step, par, rr, 2 * p2).start(priority=0)
                page_copy(step, par, rr, 2 * p2 + 1).start(priority=1)
                return 0
            lax.fori_loop(0, npages // 2, issue, 0)

    def head_sum(s):
        acc = jnp.zeros((rows, s.shape[1]), f32)
        for h in range(IDX_HEADS):
            acc = acc + jnp.maximum(s[h * rows:(h + 1) * rows, :], 0.0) * sm[:, SM_W + h:SM_W + h + 1]
        return acc

    par = pair % 2

    @pl.when(pair == 0)
    def _():
        issue_all(pair, par)

    @pl.when(pair + 1 < pl.num_programs(0))
    def _():
        issue_all(pair + 1, 1 - par)

    for rr in range(reqs):
        def drain(p, _, rr=rr):
            page_copy(pair, par, rr, p).wait()
            return 0
        lax.fori_loop(0, npages, drain, 0)

    req_of_row = row // tokens

    def group_scores(gi, _):
        p0 = gi * SCORE_GROUP_PAGES
        sc = None
        for rr in range(reqs):
            keys = jnp.concatenate([ikbuf[par, rr, p0 + j] for j in range(SCORE_GROUP_PAGES)], axis=1)
            sc_rr = head_sum(jnp.dot(iq_heads, keys.astype(bf16), preferred_element_type=f32))
            sc = sc_rr if sc is None else jnp.where(req_of_row == rr, sc_rr, sc)
        for j in range(SCORE_GROUP_PAGES):
            o_ref[0, p0 + j] = sc[:, j * PAGE_SIZE:(j + 1) * PAGE_SIZE]
        return 0

    lax.fori_loop(0, npages // SCORE_GROUP_PAGES, group_scores, 0)

    ik_new = jnp.concatenate([sm[:, SM_IK:SM_IK + IDX_DIM],
                              jnp.zeros((LANE - rows, IDX_DIM), f32)], axis=0).astype(bf16)
    s = lax.dot_general(iq_heads, ik_new, (((1,), (1,)), ((), ())), preferred_element_type=f32)
    col = lax.broadcasted_iota(i32, (1, LANE), 1)
    ok = (col < rows) & ((col // tokens) == (row // tokens)) & (col <= row)
    o_ref[0, npages] = jnp.where(ok, head_sum(s), NEG_INF)


def _ds_scores(page_table, iq, small, cache_ik_t, tokens):
    m = iq.shape[0]
    nreq, npages = page_table.shape
    rows = SUBLANE
    reqs = rows // tokens
    return pl.pallas_call(
        functools.partial(_ds_scores_kernel, tokens=tokens, npages=npages),
        out_shape=jax.ShapeDtypeStruct((m // rows, npages + 1, rows, LANE), f32),
        grid_spec=pltpu.PrefetchScalarGridSpec(
            num_scalar_prefetch=1,
            grid=(m // rows,),
            in_specs=[pl.BlockSpec((rows, IDX_HEADS * IDX_DIM), lambda i, pt: (i, 0)),
                      pl.BlockSpec((rows, LANE), lambda i, pt: (i, 0)),
                      pl.BlockSpec(memory_space=pl.ANY)],
            out_specs=pl.BlockSpec((1, npages + 1, rows, LANE), lambda i, pt: (i, 0, 0, 0)),
            scratch_shapes=[pltpu.VMEM((2, reqs, npages, IDX_DIM, PAGE_SIZE), f32),
                            pltpu.SemaphoreType.DMA((2, reqs))]),
        compiler_params=_cparams(("arbitrary",)),
        name="dsa_sample_scores",
    )(page_table.reshape(-1), iq, small, cache_ik_t)


def _ds_mask_kernel(sc_ref, o_ref, sc_s, *, topk):
    tb, p1 = sc_ref.shape[0], sc_ref.shape[1]
    sc_s[...] = sc_ref[...]

    def row_reduce(a, op):
        return op(op(a, axis=1, keepdims=True), axis=3, keepdims=True)

    def count_ge(th):
        return row_reduce(jnp.where(sc_s[...] >= th, 1.0, 0.0), jnp.sum)

    sc = sc_s[...]
    visible = sc > NEG_INF
    nvalid = row_reduce(jnp.where(visible, 1.0, 0.0), jnp.sum)
    vmax = row_reduce(sc, jnp.max)
    vmin = row_reduce(jnp.where(visible, sc, -NEG_INF), jnp.min)
    kk = jnp.minimum(float(topk), nvalid)
    lo, hi = _kth_interval(count_ge, kk, vmin, vmax, nvalid > float(topk))
    tie = count_ge(lo) > kk
    any_tie = jnp.max(jnp.where(tie, 1.0, 0.0))

    @pl.when(any_tie > 0.0)
    def _():
        need = kk - count_ge(hi)
        rr = lax.broadcasted_iota(i32, (LANE, LANE), 0)
        cc = lax.broadcasted_iota(i32, (LANE, LANE), 1)
        upper = jnp.where(rr < cc, 1.0, 0.0).astype(bf16)

        def page(p, run):
            new = []
            for t in range(tb):
                blk = sc_s[t, p]
                eq = (blk >= lo[t, 0]) & (blk < hi[t, 0])
                eqf = jnp.where(eq, 1.0, 0.0)
                rank = run[t] + jnp.dot(eqf.astype(bf16), upper, preferred_element_type=f32)
                sc_s[t, p] = jnp.where(eq & (rank >= need[t, 0]), NEG_INF, blk)
                new.append(run[t] + jnp.sum(eqf, axis=1, keepdims=True))
            return tuple(new)

        lax.fori_loop(0, p1, page, tuple(jnp.zeros((SUBLANE, 1), f32) for _ in range(tb)))

    slot = lax.broadcasted_iota(i32, (LANE, LANE * A_KV_HEADS), 0)
    lane = lax.broadcasted_iota(i32, (LANE, LANE * A_KV_HEADS), 1)
    spread = jnp.where(lane // A_KV_HEADS == slot, 1.0, 0.0).astype(bf16)

    def write_page(p):
        sel = jnp.concatenate([jnp.where(sc_s[t, p] >= lo[t, 0], 1.0, 0.0) for t in range(tb)], axis=0)
        wide = jnp.dot(sel.astype(bf16), spread, preferred_element_type=f32)
        wide = jnp.where(wide > 0.5, 0.0, NEG_INF)
        for t in range(tb):
            o_ref[t, p] = wide[t * SUBLANE:(t + 1) * SUBLANE, :]

    def write_group(gi, _):
        for j in range(MASK_WRITE_PAGES):
            write_page(gi * MASK_WRITE_PAGES + j)
        return 0

    ngroup = p1 // MASK_WRITE_PAGES
    lax.fori_loop(0, ngroup, write_group, 0)
    for p in range(ngroup * MASK_WRITE_PAGES, p1):
        write_page(p)


def _ds_mask(scores, topk, tb):
    nt, p1, rows, lanes = scores.shape
    blk = pl.BlockSpec((tb, p1, rows, lanes), lambda i: (i, 0, 0, 0))
    return pl.pallas_call(
        functools.partial(_ds_mask_kernel, topk=topk),
        out_shape=jax.ShapeDtypeStruct((nt, p1, rows, lanes * A_KV_HEADS), f32),
        grid=(nt // tb,),
        in_specs=[blk],
        out_specs=pl.BlockSpec((tb, p1, rows, lanes * A_KV_HEADS), lambda i: (i, 0, 0, 0)),
        scratch_shapes=[pltpu.VMEM((tb, p1, rows, lanes), f32)],
        compiler_params=_cparams(("parallel",)),
        name="dsa_sample_mask",
    )(scores)


def _ds_attend_kernel(pt_ref, q_ref, knew_ref, vnew_ref, mask_ref, biasl_ref, biasn_ref, az_ref,
                      kc_ref, vc_ref, o_ref, kbuf, vbuf, sem, *, tokens, npages):
    pair = pl.program_id(0)
    rows = q_ref.shape[0]
    reqs = rows // tokens
    nrow = A_HEADS * rows
    nchunk = npages // KV_CHUNK_PAGES
    total = reqs * nchunk
    page_rows = PAGE_SIZE * A_KV_HEADS
    rowi = lax.broadcasted_iota(i32, (nrow, 1), 0)
    lane = lax.broadcasted_iota(i32, (1, LANE), 1)
    kv_match = (lane % A_KV_HEADS) == (rowi // rows) // A_GROUP
    req_of_row = (rowi % rows) // tokens

    def chunk_copies(g, slot):
        cps = []
        for p in range(KV_CHUNK_PAGES):
            page = pt_ref[g * KV_CHUNK_PAGES + p]
            dst = pl.ds(p * page_rows, page_rows)
            cps.append(pltpu.make_async_copy(kc_ref.at[page], kbuf.at[slot, dst], sem.at[slot, 0]))
            cps.append(pltpu.make_async_copy(vc_ref.at[page], vbuf.at[slot, dst], sem.at[slot, 1]))
        return cps

    def start_chunk(g, slot):
        for n, cp in enumerate(chunk_copies(g, slot)):
            cp.start(priority=n % 2)

    ahead = KV_BUFFERS - 1
    first = pair * total

    assert total >= ahead

    @pl.when(pair == 0)
    def _():
        for g0 in range(ahead):
            start_chunk(g0, g0)

    def update(state, lg, vb):
        m_old, l_old, acc = state
        m_new = jnp.maximum(m_old, jnp.max(lg, axis=1, keepdims=True))
        m_safe = jnp.where(m_new == NEG_INF, 0.0, m_new)
        p = jnp.exp(lg - m_safe)
        alpha = jnp.exp(m_old - m_safe)
        l_new = alpha * l_old + jnp.sum(p, axis=1, keepdims=True)
        acc = alpha * acc + jnp.dot(p.astype(bf16), vb, preferred_element_type=f32)
        return m_new, l_new, acc

    def pad_rows(a):
        return jnp.concatenate([a, jnp.zeros((LANE - a.shape[0], a.shape[1]), a.dtype)], axis=0).astype(bf16)

    q = q_ref[...]
    q_rows = jnp.concatenate([q[:, h * A_DIM:(h + 1) * A_DIM] for h in range(A_HEADS)],
                             axis=0).astype(bf16)

    def tile_rows(a):
        return jnp.concatenate([a] * A_HEADS, axis=0)

    state = (jnp.full((nrow, 1), NEG_INF, f32), jnp.zeros((nrow, 1), f32), jnp.zeros((nrow, A_DIM), f32))
    for rr in range(reqs):
        pat = jnp.where(kv_match & (req_of_row == rr), 0.0, NEG_INF)
        pat = jnp.concatenate([pat] * (page_rows // LANE), axis=1)

        def chunk_step(c, state, bias, rr=rr, pat=pat):
            g = first + rr * nchunk + c
            slot = g % KV_BUFFERS

            @pl.when(g + ahead < pl.num_programs(0) * total)
            def _():
                start_chunk(g + ahead, (g + ahead) % KV_BUFFERS)

            for cp in chunk_copies(g, slot):
                cp.wait()
            lg = lax.dot_general(q_rows, kbuf[slot].astype(bf16), (((1,), (1,)), ((), ())),
                                 preferred_element_type=f32)
            parts = []
            for p in range(KV_CHUNK_PAGES):
                blk = lg[:, p * page_rows:(p + 1) * page_rows]
                blk = blk + (tile_rows(mask_ref[0, c * KV_CHUNK_PAGES + p]) + pat)
                if bias is not None and p == KV_CHUNK_PAGES - 1:
                    blk = blk + bias
                parts.append(blk)
            return update(state, jnp.concatenate(parts, axis=1), vbuf[slot].astype(bf16))

        state = lax.fori_loop(0, nchunk - 1, functools.partial(chunk_step, bias=None), state)
        state = chunk_step(nchunk - 1, state, biasl_ref[...])

    lg = lax.dot_general(q_rows, pad_rows(knew_ref[...]), (((1,), (1,)), ((), ())),
                         preferred_element_type=f32)
    madd = tile_rows(mask_ref[0, npages][:, 0:LANE]) + jnp.where(kv_match, 0.0, NEG_INF)
    _, l_fin, acc = update(state, lg + biasn_ref[...] + madd, pad_rows(vnew_ref[...]))
    o = acc / l_fin
    for h in range(A_HEADS):
        sl = slice(h * A_DIM, (h + 1) * A_DIM)
        o_ref[:, sl] = o[h * rows:(h + 1) * rows, :] * az_ref[:, sl]


def _ds_attend(page_table, q, k_new, v_new, mask, bias_last, bias_new, az, cache_k, cache_v, tokens):
    nreq, npages = page_table.shape
    m = q.shape[0]
    rows = SUBLANE
    chunk = KV_CHUNK_PAGES * PAGE_SIZE
    tile = lambda w: pl.BlockSpec((rows, w), lambda i, pt: (i, 0))
    const2 = lambda a: pl.BlockSpec(a.shape, lambda i, pt: (0, 0))
    new_rows = pl.BlockSpec((rows * A_KV_HEADS, A_DIM), lambda i, pt: (i, 0))
    return pl.pallas_call(
        functools.partial(_ds_attend_kernel, tokens=tokens, npages=npages),
        out_shape=jax.ShapeDtypeStruct((m, A_WIDTH), f32),
        grid_spec=pltpu.PrefetchScalarGridSpec(
            num_scalar_prefetch=1,
            grid=(m // rows,),
            in_specs=[tile(A_WIDTH), new_rows, new_rows,
                      pl.BlockSpec((1, npages + 1, rows, LANE * A_KV_HEADS), lambda i, pt: (i, 0, 0, 0)),
                      const2(bias_last), const2(bias_new), tile(A_WIDTH),
                      pl.BlockSpec(memory_space=pl.ANY),
                      pl.BlockSpec(memory_space=pl.ANY)],
            out_specs=tile(A_WIDTH),
            scratch_shapes=[pltpu.VMEM((KV_BUFFERS, chunk * A_KV_HEADS, A_DIM), f32),
                            pltpu.VMEM((KV_BUFFERS, chunk * A_KV_HEADS, A_DIM), f32),
                            pltpu.SemaphoreType.DMA((KV_BUFFERS, 2))]),
        compiler_params=_cparams(("arbitrary",)),
        name="dsa_sample_attend",
    )(page_table.reshape(-1), q, k_new, v_new, mask, bias_last, bias_new, az, cache_k, cache_v)


def _final_kernel(x_ref, hm_ref, ha_ref, sgm_ref, sga_ref, wbm_ref, wba_ref, wo_ref, o_ref):
    ym = jnp.dot(hm_ref[...].astype(bf16), wbm_ref[...], preferred_element_type=f32)
    ya = jnp.dot(ha_ref[...].astype(bf16), wba_ref[...], preferred_element_type=f32)
    merged = sgm_ref[...] * ym + sga_ref[...] * ya
    o_ref[...] = x_ref[...] + jnp.dot(merged.astype(bf16), wo_ref[...], preferred_element_type=f32)


def _final(x, hm, ha, sg, wbm, wba, wo, tm):
    m, d = x.shape
    row = lambda w: pl.BlockSpec((tm, w), lambda i: (i, 0))
    full = lambda a: pl.BlockSpec(a.shape, lambda i: (0, 0))
    return pl.pallas_call(
        _final_kernel,
        out_shape=jax.ShapeDtypeStruct((m, d), f32),
        grid=(m // tm,),
        in_specs=[row(d), row(M_WIDTH), row(A_WIDTH),
                  pl.BlockSpec((tm, d), lambda i: (i, 0)), pl.BlockSpec((tm, d), lambda i: (i, 1)),
                  full(wbm), full(wba), full(wo)],
        out_specs=row(d),
        compiler_params=_cparams(("parallel",)),
        name="merge_out",
    )(x, hm, ha, sg, sg, wbm, wba, wo)


def _segments(wt):
    widths = (("m_q", M_WIDTH), ("m_k", M_WIDTH), ("m_v", M_WIDTH), ("m_o", M_WIDTH), ("m_z", M_WIDTH),
              ("m_i", M_HEADS), ("m_f", M_HEADS),
              ("a_q", A_WIDTH), ("a_k", A_KV_WIDTH), ("a_v", A_KV_WIDTH), ("a_z", A_WIDTH),
              ("ix_q", IDX_HEADS * IDX_DIM), ("ix_k", IDX_DIM), ("ix_w", IDX_HEADS),
              ("g_m", wt.shape[1]), ("g_a", wt.shape[1]))
    seg, off = {}, 0
    for name, w in widths:
        seg[name] = wt[off:off + w]
        off += w
    assert off == wt.shape[0]
    return seg


def _rel_bucket(dist):
    max_exact = REL_BUCKETS // 2
    d = jnp.maximum(dist.astype(f32), 1.0)
    large = max_exact + jnp.floor(jnp.log(d / max_exact) / math.log(REL_MAX_DIST / max_exact)
                                  * (REL_BUCKETS - max_exact)).astype(i32)
    large = jnp.minimum(large, REL_BUCKETS - 1)
    return jnp.where(dist < max_exact, dist, large)


def _bias_of(dist, rel_bias):
    far = rel_bias[REL_BUCKETS - 1]
    onehot = jax.nn.one_hot(_rel_bucket(jnp.maximum(dist, 0)), REL_BUCKETS, dtype=f32)
    return jnp.dot(onehot, rel_bias - far, precision=lax.Precision.HIGHEST)


def kernel(x_prompt, x_sample, cache_k, cache_v, cache_idx_k, state_C, state_n, state_m, page_table,
           rel_bias, w_norm, w_in, b_if, g_mlstm, g_q, g_k, w_branch_m, w_branch_a, w_out):
    assert w_in.shape[0] == 1, "single-layer trunk"
    bsz, seq, d = x_prompt.shape
    nreq, tokens, _ = x_sample.shape
    npages = page_table.shape[1]
    past = npages * PAGE_SIZE
    assert seq % LANE == 0 and 2 * tokens == SUBLANE and (nreq * tokens) % SUBLANE == 0
    assert npages % KV_CHUNK_PAGES == 0 and npages % SCORE_GROUP_PAGES == 0 and A_GROUP == 2

    rel_bias = rel_bias.astype(f32)
    seg = _segments(jnp.swapaxes(w_in[0], 0, 1).astype(bf16))
    cast = lambda a: a.astype(bf16)
    k_scale = jnp.asarray(M_DIM ** -0.5, bf16)
    w_qkv = cast(jnp.concatenate([seg["m_q"], seg["m_k"] * k_scale, seg["m_v"]], axis=0))
    w_small = cast(jnp.concatenate(
        [seg["ix_k"], seg["m_i"], seg["m_f"], seg["ix_w"],
         jnp.zeros((LANE - IDX_DIM - 2 * M_HEADS - IDX_HEADS, d), bf16)], axis=0))
    w_g = cast(jnp.concatenate([seg["g_m"], seg["g_a"]], axis=0))
    w_o, w_z, w_az = cast(seg["m_o"]), cast(seg["m_z"]), cast(seg["a_z"])
    w_aq, w_ak, w_av, w_iq = cast(seg["a_q"]), cast(seg["a_k"]), cast(seg["a_v"]), cast(seg["ix_q"])
    gq_scaled = g_q[0].astype(f32) * (A_DIM ** -0.5)
    gk = g_k[0].astype(f32).reshape(1, A_DIM)
    bif_lane = jnp.zeros((1, LANE), f32).at[0, SM_I:SM_I + 2 * M_HEADS].set(b_if[0].astype(f32))
    wbm, wba, wo = cast(w_branch_m[0]), cast(w_branch_a[0]), cast(w_out[0])

    mp = bsz * seq
    tm = min(1024, mp)
    tn = 512
    xp = x_prompt.reshape(mp, d)
    xn = _rmsnorm(xp, w_norm[0], min(512, mp))
    (qkv,) = _proj_nat(_proj_cast_kernel, xn, [w_qkv], [], [bf16], tm, tn, "proj_mqkv")
    small, ik_bf = _proj_nat(_proj_dual_kernel, xn, [w_small], [], [f32, bf16], tm, LANE, "proj_small")
    (poz,) = _proj_nat(_proj_oz_kernel, xn, [w_o, w_z], [], [f32], tm, tn, "proj_moz")
    k_rows, k_bf = _proj_nat(_proj_knorm_kernel, xn, [w_ak], [gk], [f32, bf16], tm, tn, "proj_ak",
                             rows3d=(0,))
    (v_rows,) = _proj_nat(_proj_rows_kernel, xn, [w_av], [], [f32], tm, tn, "proj_av", rows3d=(0,))
    (az,) = _proj_nat(functools.partial(_proj_act_kernel, act="silu"), xn, [w_az], [], [f32],
                      tm, tn, "proj_az")
    (sg,) = _proj_nat(functools.partial(_proj_act_kernel, act="sigmoid"), xn, [w_g], [], [f32],
                      tm, tn, "proj_gates")
    tmt = min(512, mp)
    gq_tile = jnp.broadcast_to(jnp.tile(gq_scaled * LOG2E, A_HEADS)[:, None], (A_WIDTH, LANE))
    qt = _proj_T(xn, w_aq, gq_tile, "qnorm", bf16, tmt, 512, "projT_aq")
    vt = _proj_T(xn, w_av, None, "cast", bf16, tmt, 512, "projT_av")
    iqt = _proj_T(xn, w_iq, None, "cast", bf16, tmt, IDX_HEADS * IDX_DIM, "projT_iq")
    smallt = _proj_T(xn, w_small, None, "cast", f32, tmt, LANE, "projT_small")

    chunk = 256 if seq % 256 == 0 else LANE
    hm, c_p, nm_p = _mlstm_prompt(qkv, small, poz, bif_lane, g_mlstm[0].astype(f32), bsz, seq, chunk)

    nsub = 2 if seq % (2 * LANE) == 0 else 1
    tb = nsub * LANE
    assert REL_MAX_DIST <= tb + 1
    tl = jnp.arange(LANE)
    tlb = jnp.arange(tb)
    dist0 = tlb[None, :] - tlb[:, None]
    bias_tiles = jnp.stack([_bias_of(dist0, rel_bias), _bias_of(dist0 + tb, rel_bias)]) * LOG2E
    bias_tiles = bias_tiles.reshape(2, tb, tb, A_KV_HEADS, A_GROUP)
    bias_tiles = jnp.transpose(bias_tiles, (0, 3, 1, 4, 2)).reshape(2, A_KV_HEADS, tb, A_GROUP * tb)
    topk_p = min(TOPK_MAX, seq // 4)
    ha = _dsa_prompt(k_bf, vt, ik_bf, qt, iqt, smallt, bias_tiles, az, bsz, seq, topk_p, nsub)
    y_p = _final(xp, hm, ha, sg, wbm, wba, wo, min(512, mp))

    ms = nreq * tokens
    xs = x_sample.reshape(ms, d)
    tms = min(512, ms)
    xns = _rmsnorm(xs, w_norm[0], tms)
    nat = lambda body, ws, ex, outs, name, tn_=tn, **kw: _proj_nat(body, xns, ws, ex, outs, tms, tn_, name, **kw)
    (qkv_s,) = nat(_proj_cast_kernel, [w_qkv], [], [f32], "sproj_mqkv")
    (small_s,) = nat(functools.partial(_proj_act_kernel, act=None), [w_small], [], [f32], "sproj_small", LANE)
    (poz_s,) = nat(_proj_oz_kernel, [w_o, w_z], [], [f32], "sproj_moz")
    (q_s,) = nat(_proj_qnorm_kernel, [w_aq], [gq_scaled.reshape(1, A_DIM)], [f32], "sproj_aq")
    k_rows_s, _ = nat(_proj_knorm_kernel, [w_ak], [gk], [f32, bf16], "sproj_ak", rows3d=(0,))
    (v_rows_s,) = nat(_proj_rows_kernel, [w_av], [], [f32], "sproj_av", rows3d=(0,))
    (az_s,) = nat(functools.partial(_proj_act_kernel, act="silu"), [w_az], [], [f32], "sproj_az")
    (sg_s,) = nat(functools.partial(_proj_act_kernel, act="sigmoid"), [w_g], [], [f32], "sproj_gates")
    (iq_s,) = nat(functools.partial(_proj_act_kernel, act=None), [w_iq], [], [f32], "sproj_iq",
                  IDX_HEADS * IDX_DIM)

    hm_s, c_s, n_s, m_s = _mlstm_sample(qkv_s, small_s, poz_s, bif_lane, g_mlstm[0].astype(f32),
                                        state_C[0].astype(f32), state_n[0].astype(f32),
                                        state_m[0].astype(f32), tokens)

    cache_ik_t = jnp.swapaxes(cache_idx_k[0].astype(f32), 1, 2)
    scores_s = _ds_scores(page_table, iq_s, small_s, cache_ik_t, tokens)
    topk_s = min(TOPK_MAX, (past + tokens) // 4)
    ntile = ms // SUBLANE
    mask_s = _ds_mask(scores_s, topk_s, math.gcd(ntile, 8))
    tok_of_row = jnp.arange(SUBLANE) % tokens

    def logit_layout(b):
        b = jnp.transpose(b, (2, 0, 1))[..., None]
        b = jnp.broadcast_to(b, b.shape[:3] + (A_KV_HEADS,))
        return b.reshape(A_HEADS * SUBLANE, -1)

    dist_last = (PAGE_SIZE + tok_of_row)[:, None] - tl[None, :]
    bias_last = logit_layout(_bias_of(dist_last, rel_bias))
    dist_new = tok_of_row[:, None] - tok_of_row[None, :]
    bias_new = logit_layout(_bias_of(dist_new, rel_bias))
    bias_new = jnp.pad(bias_new, ((0, 0), (0, LANE - bias_new.shape[1])))
    flat_rows = lambda a: a.reshape(ms * A_KV_HEADS, A_DIM)
    o_s = _ds_attend(page_table, q_s, flat_rows(k_rows_s), flat_rows(v_rows_s), mask_s, bias_last,
                     bias_new, az_s,
                     cache_k[0].astype(f32).reshape(-1, PAGE_SIZE * A_KV_HEADS, A_DIM),
                     cache_v[0].astype(f32).reshape(-1, PAGE_SIZE * A_KV_HEADS, A_DIM), tokens)
    y_s = _final(xs, hm_s, o_s, sg_s, wbm, wba, wo, tms)

    ck, cv, ci = cache_k.dtype, cache_v.dtype, cache_idx_k.dtype
    kv_shape_p = (1, bsz, seq, A_KV_HEADS, A_DIM)
    kv_shape_s = (1, nreq, tokens, A_KV_HEADS, A_DIM)
    return (y_p.reshape(bsz, seq, d).astype(x_prompt.dtype),
            y_s.reshape(nreq, tokens, d).astype(x_sample.dtype),
            k_rows.reshape(kv_shape_p).astype(ck),
            v_rows.reshape(kv_shape_p).astype(cv),
            small[:, SM_IK:SM_IK + IDX_DIM].reshape(1, bsz, seq, IDX_DIM).astype(ci),
            c_p.reshape(1, bsz, M_HEADS, M_DIM, M_DIM).astype(state_C.dtype),
            nm_p[:, :, 0, :].reshape(1, bsz, M_HEADS, M_DIM).astype(state_n.dtype),
            nm_p[:, :, 1, 0].reshape(1, bsz, M_HEADS).astype(state_m.dtype),
            k_rows_s.reshape(kv_shape_s).astype(ck),
            v_rows_s.reshape(kv_shape_s).astype(cv),
            small_s[:, SM_IK:SM_IK + IDX_DIM].reshape(1, nreq, tokens, IDX_DIM).astype(ci),
            c_s.reshape(1, nreq, M_HEADS, M_DIM, M_DIM).astype(state_C.dtype),
            n_s.reshape(1, nreq, M_HEADS, M_DIM).astype(state_n.dtype),
            m_s.reshape(1, nreq, M_HEADS).astype(state_m.dtype))
```
